```python
import jax, jax.numpy as jnp
from jax import lax
import numpy as np

D_MODEL = 1024
BATCH = 32
SEQ = 2048
DEPTH = 4

CHUNK = 64
N_META = 16
Q_BLOCK = 128
PAD = Q_BLOCK - N_META
N_A_LAYERS = DEPTH // 2
N_B_LAYERS = DEPTH - N_A_LAYERS
RMS_EPS = 1e-6
GN_EPS = 1e-5
ROPE_THETA = 10000.0
MASK_VALUE = -1e30

RET_HEADS = 4
RET_QK_DIM = D_MODEL // RET_HEADS
RET_V_DIM = 2 * D_MODEL // RET_HEADS
RET_IN = 2 * RET_HEADS * RET_QK_DIM + 2 * RET_HEADS * RET_V_DIM

MLA_HEADS = 8
MLA_NOPE = 128
MLA_ROPE = 64
MLA_V = 128
MLA_Q_RANK = 768
MLA_KV_RANK = 512

FFN_HIDDEN = ((8 * D_MODEL + 3 * 256 - 1) // (3 * 256)) * 256

kernel_name = "yoco_retention_mla_meta_chunk_causal_trunk"


def rmsnorm(x, g):
    xf = x.astype(jnp.float32)
    y = xf * lax.rsqrt(jnp.mean(xf * xf, axis=-1, keepdims=True) + RMS_EPS)
    return (y * g.astype(jnp.float32)).astype(x.dtype)


def rope_tables(pos, dim):
    inv = 1.0 / (ROPE_THETA ** (jnp.arange(0, dim, 2, dtype=jnp.float32) / dim))
    ang = pos.astype(jnp.float32)[:, None] * inv[None, :]
    return jnp.cos(ang), jnp.sin(ang)


def apply_rope(x, cos, sin):
    xf = x.astype(jnp.float32)
    half = x.shape[-1] // 2
    x1, x2 = xf[..., :half], xf[..., half:]
    c = cos[None, :, None, :]
    s = sin[None, :, None, :]
    return jnp.concatenate([x1 * c - x2 * s, x1 * s + x2 * c], axis=-1).astype(x.dtype)


def retention_mixer(hn, w_in, gn_g, w_o, cos, sin, valid):
    B, L, _ = hn.shape
    H, DK, DV, C = RET_HEADS, RET_QK_DIM, RET_V_DIM, CHUNK
    NC = L // C
    dt = hn.dtype
    proj = hn @ w_in
    q = proj[..., :H * DK].reshape(B, L, H, DK)
    k = proj[..., H * DK:2 * H * DK].reshape(B, L, H, DK)
    v = proj[..., 2 * H * DK:2 * H * DK + H * DV].reshape(B, L, H, DV)
    gate = proj[..., 2 * H * DK + H * DV:]
    q = apply_rope(q, cos, sin)
    k = apply_rope(k, cos, sin) * (DK ** -0.5) * valid[None, :, None, None]

    def to_chunks(t):
        return t.reshape(B, NC, C, H, t.shape[-1]).transpose(1, 0, 3, 2, 4)

    log_g = jnp.log1p(-jnp.exp2(-5.0 - jnp.arange(H, dtype=jnp.float32)))
    idx = jnp.arange(C, dtype=jnp.float32)
    intra = jnp.exp(log_g[:, None, None] * jnp.abs(idx[:, None] - idx[None, :])).astype(dt)
    q_dec = jnp.exp(log_g[:, None] * (idx + 1.0)).astype(dt)
    k_dec = jnp.exp(log_g[:, None] * (C - 1.0 - idx)).astype(dt)
    chunk_dec = jnp.exp(log_g * C).astype(dt)

    def step(S, qkv):
        qc, kc, vc = qkv
        sc = jnp.einsum('bhid,bhjd->bhij', qc, kc) * intra
        o = (jnp.einsum('bhij,bhjv->bhiv', sc, vc)
             + jnp.einsum('bhid,bhdv->bhiv', qc * q_dec[:, :, None], S))
        S = S * chunk_dec[:, None, None] + jnp.einsum('bhjd,bhjv->bhdv', kc * k_dec[:, :, None], vc)
        return S, o

    S0 = jnp.zeros((B, H, DK, DV), dt)
    _, o = lax.scan(step, S0, (to_chunks(q), to_chunks(k), to_chunks(v)))
    o = o.transpose(1, 0, 3, 2, 4).reshape(B, L, H, DV)
    of = o.astype(jnp.float32)
    mu = jnp.mean(of, axis=-1, keepdims=True)
    var = jnp.mean(jnp.square(of - mu), axis=-1, keepdims=True)
    on = ((of - mu) * lax.rsqrt(var + GN_EPS)).reshape(B, L, H * DV) * gn_g.astype(jnp.float32)
    return (jax.nn.silu(gate) * on.astype(dt)) @ w_o


def mla_shared_kv(h, norm_kv_g, w_kv_a, kv_a_norm_g, w_kv_b, cos, sin):
    B, L, _ = h.shape
    kv_a = rmsnorm(h, norm_kv_g) @ w_kv_a
    c_kv = rmsnorm(kv_a[..., :MLA_KV_RANK], kv_a_norm_g)
    k_rope = apply_rope(kv_a[..., MLA_KV_RANK:][:, :, None, :], cos, sin)
    kv = (c_kv @ w_kv_b).reshape(B, L, MLA_HEADS, MLA_NOPE + MLA_V)
    k = jnp.concatenate([kv[..., :MLA_NOPE],
                         jnp.broadcast_to(k_rope, (B, L, MLA_HEADS, MLA_ROPE))], axis=-1)
    return k, kv[..., MLA_NOPE:]


def mla_mixer(hn, w_q_a, q_a_norm_g, w_q_b, w_o, k, v, cos, sin, chunk_id, valid_key):
    B, L, _ = hn.shape
    cq = rmsnorm(hn @ w_q_a, q_a_norm_g)
    q = (cq @ w_q_b).reshape(B, L, MLA_HEADS, MLA_NOPE + MLA_ROPE)
    q = jnp.concatenate([q[..., :MLA_NOPE], apply_rope(q[..., MLA_NOPE:], cos, sin)], axis=-1)
    q = q * ((MLA_NOPE + MLA_ROPE) ** -0.5)
    outs = []
    for qb in range(L // Q_BLOCK):
        s, e = qb * Q_BLOCK, (qb + 1) * Q_BLOCK
        sc = jnp.einsum('bqhd,bkhd->bhqk', q[:, s:e], k[:, :e]).astype(jnp.float32)
        mask = (chunk_id[None, :e] <= chunk_id[s:e, None]) & valid_key[None, :e]
        sc = jnp.where(mask[None, None], sc, MASK_VALUE)
        p = jax.nn.softmax(sc, axis=-1).astype(v.dtype)
        outs.append(jnp.einsum('bhqk,bkhd->bqhd', p, v[:, :e]))
    o = jnp.concatenate(outs, axis=1).reshape(B, L, MLA_HEADS * MLA_V)
    return o @ w_o


def swiglu(hn, w1, w3, w2):
    return (jax.nn.silu(hn @ w1) * (hn @ w3)) @ w2


def setup_inputs(seed: int = 0) -> dict:
    key = jax.random.key(seed)
    ks = jax.random.split(key, 24)
    f32 = jnp.float32
    res = (2 * DEPTH) ** -0.5

    def w(k, shape, fan_in, scale=1.0):
        return jax.random.normal(k, shape, f32) * (fan_in ** -0.5) * scale

    def gain(k, shape):
        return 1.0 + 0.02 * jax.random.normal(k, shape, f32)

    return {
        "x": jax.random.normal(ks[0], (BATCH, SEQ, D_MODEL), f32),
        "meta": jax.random.normal(ks[1], (N_META, D_MODEL), f32),
        "norm_mix_g": gain(ks[2], (DEPTH, D_MODEL)),
        "norm_ffn_g": gain(ks[3], (DEPTH, D_MODEL)),
        "ret_w_in": w(ks[4], (N_A_LAYERS, D_MODEL, RET_IN), D_MODEL),
        "ret_gn_g": gain(ks[5], (N_A_LAYERS, RET_HEADS * RET_V_DIM)),
        "ret_w_o": w(ks[6], (N_A_LAYERS, RET_HEADS * RET_V_DIM, D_MODEL), RET_HEADS * RET_V_DIM, res),
        "mla_norm_kv_g": gain(ks[7], (D_MODEL,)),
        "mla_w_kv_a": w(ks[8], (D_MODEL, MLA_KV_RANK + MLA_ROPE), D_MODEL),
        "mla_kv_a_norm_g": gain(ks[9], (MLA_KV_RANK,)),
        "mla_w_kv_b": w(ks[10], (MLA_KV_RANK, MLA_HEADS * (MLA_NOPE + MLA_V)), MLA_KV_RANK),
        "mla_w_q_a": w(ks[11], (N_B_LAYERS, D_MODEL, MLA_Q_RANK), D_MODEL),
        "mla_q_a_norm_g": gain(ks[12], (N_B_LAYERS, MLA_Q_RANK)),
        "mla_w_q_b": w(ks[13], (N_B_LAYERS, MLA_Q_RANK, MLA_HEADS * (MLA_NOPE + MLA_ROPE)), MLA_Q_RANK),
        "mla_w_o": w(ks[14], (N_B_LAYERS, MLA_HEADS * MLA_V, D_MODEL), MLA_HEADS * MLA_V, res),
        "ffn_w1": w(ks[15], (DEPTH, D_MODEL, FFN_HIDDEN), D_MODEL),
        "ffn_w3": w(ks[16], (DEPTH, D_MODEL, FFN_HIDDEN), D_MODEL),
        "ffn_w2": w(ks[17], (DEPTH, FFN_HIDDEN, D_MODEL), FFN_HIDDEN, res),
        "final_g": gain(ks[18], (D_MODEL,)),
    }


def reference(x, meta, norm_mix_g, norm_ffn_g, ret_w_in, ret_gn_g, ret_w_o,
              mla_norm_kv_g, mla_w_kv_a, mla_kv_a_norm_g, mla_w_kv_b,
              mla_w_q_a, mla_q_a_norm_g, mla_w_q_b, mla_w_o,
              ffn_w1, ffn_w3, ffn_w2, final_g):
    B, S, D = x.shape
    L = PAD + N_META + S
    h = jnp.concatenate([jnp.zeros((B, PAD, D), x.dtype),
                         jnp.broadcast_to(meta[None].astype(x.dtype), (B, N_META, D)),
                         x], axis=1)
    slot = jnp.arange(L)
    chunk_id = slot // CHUNK
    valid_key = slot >= PAD
    valid = valid_key.astype(x.dtype)
    pos = slot - PAD
    cos_r, sin_r = rope_tables(pos, RET_QK_DIM)
    cos_m, sin_m = rope_tables(pos, MLA_ROPE)

    k_sh = None
    v_sh = None
    for layer in range(DEPTH):
        hn = rmsnorm(h, norm_mix_g[layer])
        if layer < N_A_LAYERS:
            h = h + retention_mixer(hn, ret_w_in[layer], ret_gn_g[layer], ret_w_o[layer],
                                    cos_r, sin_r, valid)
        else:
            if layer == N_A_LAYERS:
                k_sh, v_sh = mla_shared_kv(h, mla_norm_kv_g, mla_w_kv_a, mla_kv_a_norm_g,
                                           mla_w_kv_b, cos_m, sin_m)
            j = layer - N_A_LAYERS
            h = h + mla_mixer(hn, mla_w_q_a[j], mla_q_a_norm_g[j], mla_w_q_b[j], mla_w_o[j],
                              k_sh, v_sh, cos_m, sin_m, chunk_id, valid_key)
        h = h + swiglu(rmsnorm(h, norm_ffn_g[layer]), ffn_w1[layer], ffn_w3[layer], ffn_w2[layer])
    return rmsnorm(h, final_g)[:, PAD + N_META:]
```

```python
import functools

import jax
import jax.numpy as jnp
from jax import lax
from jax.experimental import pallas as pl
from jax.experimental.pallas import tpu as pltpu

F32 = jnp.float32
BF16 = jnp.bfloat16

CHUNK = 64
N_META = 16
Q_BLOCK = 128
PAD = Q_BLOCK - N_META
RMS_EPS = 1e-6
GN_EPS = 1e-5
ROPE_THETA = 10000.0
MASK_VALUE = -1e30

RET_HEADS = 4
RET_QK_DIM = 256
RET_V_DIM = 512

MLA_HEADS = 8
MLA_NOPE = 128
MLA_ROPE = 64
MLA_V = 128
MLA_KV_RANK = 512

FFN_CHUNK = 256
HEAD_BLOCK = 128
SEQ_BLOCK = 256
VMEM_LIMIT = 56 * 1024 * 1024


def _params(n_grid):
    return pltpu.CompilerParams(
        dimension_semantics=("arbitrary",) * n_grid, vmem_limit_bytes=VMEM_LIMIT)


def _resident(shape):
    return pl.BlockSpec(shape, lambda *_: (0,) * len(shape), pipeline_mode=pl.Buffered(1))


def _rms(x, g):
    return x * lax.rsqrt(jnp.mean(x * x, axis=-1, keepdims=True) + RMS_EPS) * g


def _dot(a, b):
    return jnp.dot(a, b, preferred_element_type=F32)


def _dot_nt(a, b):
    return lax.dot_general(a, b, (((1,), (1,)), ((), ())), preferred_element_type=F32)


def _dot_tn(a, b):
    return lax.dot_general(a, b, (((0,), (0,)), ((), ())), preferred_element_type=F32)


def _proj_kernel(h_ref, g_ref, w_ref, o_ref, *, tn):
    hn = _rms(h_ref[0], g_ref[...]).astype(BF16)
    for c in range(w_ref.shape[1] // tn):
        o_ref[0, :, c * tn:(c + 1) * tn] = _dot(hn, w_ref[:, c * tn:(c + 1) * tn]).astype(BF16)


def _proj(h, g, w, tl, tn=1024):
    B, L, D = h.shape
    N = w.shape[1]
    return pl.pallas_call(
        functools.partial(_proj_kernel, tn=tn),
        grid=(B, L // tl),
        in_specs=[pl.BlockSpec((1, tl, D), lambda b, i: (b, i, 0)),
                  _resident((1, D)), _resident((D, N))],
        out_specs=pl.BlockSpec((1, tl, N), lambda b, i: (b, i, 0)),
        out_shape=jax.ShapeDtypeStruct((B, L, N), BF16),
        compiler_params=_params(2), name="ret_proj",
    )(h, g, w)


def _rope_halves(x, c, s):
    half = x.shape[1] // 2
    x1, x2 = x[:, :half], x[:, half:]
    return jnp.concatenate([x1 * c - x2 * s, x1 * s + x2 * c], axis=1)


def _retention_kernel(q_ref, k_ref, v_ref, gate_ref, cq_ref, sq_ref, ck_ref, sk_ref,
                      dmat_ref, qdec_ref, kdec_ref, gn_ref, o_ref, s_ref):
    s_ref[...] = jnp.zeros_like(s_ref)

    def block(r0, n):
        rows = pl.ds(r0, n)
        off = SEQ_BLOCK - n
        q = _rope_halves(q_ref[0, rows, :].astype(F32), cq_ref[rows, :], sq_ref[rows, :])
        k = _rope_halves(k_ref[0, rows, :].astype(F32), ck_ref[rows, :], sk_ref[rows, :])
        v = v_ref[0, rows, :]
        sc = _dot_nt(q.astype(BF16), k.astype(BF16)) * dmat_ref[0, :n, :n]
        state = s_ref[...]
        qd = (q * qdec_ref[0, :n, :]).astype(BF16)
        o = _dot(sc.astype(BF16), v) + _dot(qd, state.astype(BF16))
        kd = (k * kdec_ref[0, off:off + n, :]).astype(BF16)
        s_ref[...] = state * qdec_ref[0, n - 1:n, :] + _dot_tn(kd, v)
        mu = jnp.mean(o, axis=-1, keepdims=True)
        d = o - mu
        var = jnp.mean(d * d, axis=-1, keepdims=True)
        on = d * lax.rsqrt(var + GN_EPS) * gn_ref[...]
        g = gate_ref[0, rows, :].astype(F32)
        o_ref[0, rows, :] = (g * jax.nn.sigmoid(g) * on).astype(BF16)

    block(0, HEAD_BLOCK)

    def body(i, carry):
        block(pl.multiple_of(HEAD_BLOCK + i * SEQ_BLOCK, HEAD_BLOCK), SEQ_BLOCK)
        return carry

    lax.fori_loop(0, (q_ref.shape[1] - HEAD_BLOCK) // SEQ_BLOCK, body, 0)


def _retention(proj, tabs, gn):
    B, L, _ = proj.shape
    H, DK, DV = RET_HEADS, RET_QK_DIM, RET_V_DIM
    nq = H * DK // DK
    nv = 2 * H * DK // DV
    ng = nv + H
    return pl.pallas_call(
        _retention_kernel,
        grid=(B, H),
        in_specs=[pl.BlockSpec((1, L, DK), lambda b, h: (b, 0, h)),
                  pl.BlockSpec((1, L, DK), lambda b, h: (b, 0, nq + h)),
                  pl.BlockSpec((1, L, DV), lambda b, h: (b, 0, nv + h)),
                  pl.BlockSpec((1, L, DV), lambda b, h: (b, 0, ng + h)),
                  _resident((L, DK // 2)), _resident((L, DK // 2)),
                  _resident((L, DK // 2)), _resident((L, DK // 2)),
                  pl.BlockSpec((1, SEQ_BLOCK, SEQ_BLOCK), lambda b, h: (h, 0, 0)),
                  pl.BlockSpec((1, SEQ_BLOCK, 1), lambda b, h: (h, 0, 0)),
                  pl.BlockSpec((1, SEQ_BLOCK, 1), lambda b, h: (h, 0, 0)),
                  pl.BlockSpec((1, DV), lambda b, h: (0, h))],
        out_specs=pl.BlockSpec((1, L, DV), lambda b, h: (b, 0, h)),
        out_shape=jax.ShapeDtypeStruct((B, L, H * DV), BF16),
        scratch_shapes=[pltpu.VMEM((DK, DV), F32)],
        compiler_params=_params(2), name="retention",
    )(proj, proj, proj, proj, tabs["cos_q"], tabs["sin_q"], tabs["cos_k"], tabs["sin_k"],
      tabs["dmat"], tabs["qdec"], tabs["kdec"], gn)


def _out_ffn_kernel(h_ref, y_ref, wo_ref, g_ref, w1_ref, w3_ref, w2_ref, *rest, final):
    if final:
        gf_ref, o_ref = rest
    else:
        (o_ref,) = rest
    h1 = h_ref[0] + _dot(y_ref[0], wo_ref[...])
    hn = _rms(h1, g_ref[...]).astype(BF16)
    acc = h1
    for c in range(w1_ref.shape[1] // FFN_CHUNK):
        cols = slice(c * FFN_CHUNK, (c + 1) * FFN_CHUNK)
        a = _dot(hn, w1_ref[:, cols])
        u = (a * jax.nn.sigmoid(a) * _dot(hn, w3_ref[:, cols])).astype(BF16)
        acc = acc + _dot(u, w2_ref[cols, :])
    if final:
        acc = _rms(acc, gf_ref[...])
    o_ref[0] = acc


def _out_ffn(h, y, wo, g, w1, w3, w2, tl, final_g=None):
    B, L, D = h.shape
    Y = y.shape[2]
    F = w1.shape[1]
    final = final_g is not None
    in_specs = [pl.BlockSpec((1, tl, D), lambda b, i: (b, i, 0)),
                pl.BlockSpec((1, tl, Y), lambda b, i: (b, i, 0)),
                _resident((Y, D)), _resident((1, D)),
                _resident((D, F)), _resident((D, F)), _resident((F, D))]
    args = [h, y, wo, g, w1, w3, w2]
    if final:
        in_specs.append(_resident((1, D)))
        args.append(final_g)
    return pl.pallas_call(
        functools.partial(_out_ffn_kernel, final=final),
        grid=(B, L // tl),
        in_specs=in_specs,
        out_specs=pl.BlockSpec((1, tl, D), lambda b, i: (b, i, 0)),
        out_shape=jax.ShapeDtypeStruct((B, L, D), F32),
        compiler_params=_params(2), name="out_ffn",
    )(*args)


def _rope_pair(t, cs):
    u = t * cs
    return u + pltpu.roll(u, MLA_ROPE, 1)


def _mla_kv_kernel(h_ref, g_ref, wa_ref, ga_ref, wb_ref, cs_ref, kv_ref, kr_ref):
    hn = _rms(h_ref[0], g_ref[...]).astype(BF16)
    kva = _dot(hn, wa_ref[...])
    ckv = _rms(kva[:, :MLA_KV_RANK], ga_ref[...]).astype(BF16)
    rp = _rope_pair(kva[:, MLA_KV_RANK:], cs_ref[...])
    lane = lax.broadcasted_iota(jnp.int32, rp.shape, 1)
    kr_ref[0] = jnp.where(lane < MLA_ROPE, rp, 0.0).astype(BF16)
    kv_ref[0] = _dot(ckv, wb_ref[...]).astype(BF16)


def _mla_kv(h, g, wa, ga, wb, cs, tl):
    B, L, D = h.shape
    NA, NB = wa.shape[1], wb.shape[1]
    return pl.pallas_call(
        _mla_kv_kernel,
        grid=(B, L // tl),
        in_specs=[pl.BlockSpec((1, tl, D), lambda b, i: (b, i, 0)),
                  _resident((1, D)), _resident((D, NA)), _resident((1, MLA_KV_RANK)),
                  _resident((MLA_KV_RANK, NB)),
                  pl.BlockSpec((tl, 2 * MLA_ROPE), lambda b, i: (i, 0))],
        out_specs=[pl.BlockSpec((1, tl, NB), lambda b, i: (b, i, 0)),
                   pl.BlockSpec((1, tl, 2 * MLA_ROPE), lambda b, i: (b, i, 0))],
        out_shape=[jax.ShapeDtypeStruct((B, L, NB), BF16),
                   jax.ShapeDtypeStruct((B, L, 2 * MLA_ROPE), BF16)],
        compiler_params=_params(2), name="mla_kv",
    )(h, g, wa, ga, wb, cs)


def _mla_q_kernel(h_ref, g_ref, wa_ref, ga_ref, wb_ref, cs_ref, q_ref):
    hn = _rms(h_ref[0], g_ref[...]).astype(BF16)
    cq = _rms(_dot(hn, wa_ref[...]), ga_ref[...]).astype(BF16)
    scale = (MLA_NOPE + MLA_ROPE) ** -0.5
    width = MLA_NOPE + 2 * MLA_ROPE
    for hd in range(MLA_HEADS):
        qh = _dot(cq, wb_ref[:, hd * width:(hd + 1) * width])
        q_ref[0, :, hd * width:hd * width + MLA_NOPE] = (qh[:, :MLA_NOPE] * scale).astype(BF16)
        q_ref[0, :, hd * width + MLA_NOPE:(hd + 1) * width] = (
            _rope_pair(qh[:, MLA_NOPE:], cs_ref[...]) * scale).astype(BF16)


def _mla_q(h, g, wa, ga, wb, cs, tl):
    B, L, D = h.shape
    R, N = wb.shape
    return pl.pallas_call(
        _mla_q_kernel,
        grid=(B, L // tl),
        in_specs=[pl.BlockSpec((1, tl, D), lambda b, i: (b, i, 0)),
                  _resident((1, D)), _resident((D, R)), _resident((1, R)), _resident((R, N)),
                  pl.BlockSpec((tl, 2 * MLA_ROPE), lambda b, i: (i, 0))],
        out_specs=pl.BlockSpec((1, tl, N), lambda b, i: (b, i, 0)),
        out_shape=jax.ShapeDtypeStruct((B, L, N), BF16),
        compiler_params=_params(2), name="mla_q",
    )(h, g, wa, ga, wb, cs)


def _attention_kernel(q_ref, kn_ref, kr_ref, v_ref, o_ref, kcat_ref):
    kcat_ref[:, :MLA_NOPE] = kn_ref[0]
    kcat_ref[:, MLA_NOPE:] = kr_ref[0]

    def update(carry, s, v):
        m, l, acc = carry
        m_new = jnp.maximum(m, jnp.max(s, axis=1, keepdims=True))
        alpha = jnp.exp(m - m_new)
        p = jnp.exp(s - m_new)
        return (m_new, alpha * l + jnp.sum(p, axis=1, keepdims=True),
                alpha * acc + _dot(p.astype(BF16), v))

    def chunk_mask(n):
        row = lax.broadcasted_iota(jnp.int32, (n, n), 0)
        col = lax.broadcasted_iota(jnp.int32, (n, n), 1)
        return (col // CHUNK) <= (row // CHUNK), col

    def init(n):
        return (jnp.full((n, 1), MASK_VALUE, F32), jnp.zeros((n, 1), F32),
                jnp.zeros((n, MLA_V), F32))

    def finish(rows, carry):
        _, l, acc = carry
        o_ref[0, rows, :] = (acc / l).astype(BF16)

    rows0 = pl.ds(0, HEAD_BLOCK)
    mask0, col0 = chunk_mask(HEAD_BLOCK)
    s = _dot_nt(q_ref[0, rows0, :], kcat_ref[rows0, :])
    s = jnp.where(mask0 & (col0 >= PAD), s, MASK_VALUE)
    finish(rows0, update(init(HEAD_BLOCK), s, v_ref[0, rows0, :]))

    def qblock(i, carry_unused):
        r0 = pl.multiple_of(HEAD_BLOCK + i * SEQ_BLOCK, HEAD_BLOCK)
        rows = pl.ds(r0, SEQ_BLOCK)
        q = q_ref[0, rows, :]
        mask, _ = chunk_mask(SEQ_BLOCK)
        s = jnp.where(mask, _dot_nt(q, kcat_ref[rows, :]), MASK_VALUE)
        carry = update(init(SEQ_BLOCK), s, v_ref[0, rows, :])
        col = lax.broadcasted_iota(jnp.int32, (SEQ_BLOCK, HEAD_BLOCK), 1)
        s = jnp.where(col >= PAD, _dot_nt(q, kcat_ref[rows0, :]), MASK_VALUE)
        carry = update(carry, s, v_ref[0, rows0, :])

        def past(j, c):
            kr0 = pl.multiple_of(HEAD_BLOCK + j * SEQ_BLOCK, HEAD_BLOCK)
            krows = pl.ds(kr0, SEQ_BLOCK)
            return update(c, _dot_nt(q, kcat_ref[krows, :]), v_ref[0, krows, :])

        finish(rows, lax.fori_loop(0, i, past, carry))
        return carry_unused

    lax.fori_loop(0, (q_ref.shape[1] - HEAD_BLOCK) // SEQ_BLOCK, qblock, 0)


def _attention(q, kv, kr):
    B, L, _ = q.shape
    H = MLA_HEADS
    W = MLA_NOPE + 2 * MLA_ROPE
    return pl.pallas_call(
        _attention_kernel,
        grid=(B, H),
        in_specs=[pl.BlockSpec((1, L, W), lambda b, h: (b, 0, h)),
                  pl.BlockSpec((1, L, MLA_NOPE), lambda b, h: (b, 0, 2 * h)),
                  pl.BlockSpec((1, L, 2 * MLA_ROPE), lambda b, h: (b, 0, 0)),
                  pl.BlockSpec((1, L, MLA_V), lambda b, h: (b, 0, 2 * h + 1))],
        out_specs=pl.BlockSpec((1, L, MLA_V), lambda b, h: (b, 0, h)),
        out_shape=jax.ShapeDtypeStruct((B, L, H * MLA_V), BF16),
        scratch_shapes=[pltpu.VMEM((L, W), BF16)],
        compiler_params=_params(2), name="attention",
    )(q, kv, kr, kv)


def _rope_tables(pos, dim):
    inv = 1.0 / (ROPE_THETA ** (jnp.arange(0, dim, 2, dtype=F32) / dim))
    ang = pos.astype(F32)[:, None] * inv[None, :]
    return jnp.cos(ang), jnp.sin(ang)


def _tables(L):
    slot = jnp.arange(L)
    pos = slot - PAD
    valid = (slot >= PAD).astype(F32)[:, None]
    cos_r, sin_r = _rope_tables(pos, RET_QK_DIM)
    kscale = valid * (RET_QK_DIM ** -0.5)
    cos_m, sin_m = _rope_tables(pos, MLA_ROPE)
    log_g = jnp.log1p(-jnp.exp2(-5.0 - jnp.arange(RET_HEADS, dtype=F32)))[:, None, None]
    idx = jnp.arange(SEQ_BLOCK, dtype=F32)
    ci = jnp.arange(SEQ_BLOCK) // CHUNK
    dist = jnp.abs(idx[:, None] - idx[None, :])
    dmat = jnp.where((ci[None, :] <= ci[:, None])[None], jnp.exp(log_g * dist[None]), 0.0)
    return {
        "cos_q": cos_r, "sin_q": sin_r, "cos_k": cos_r * kscale, "sin_k": sin_r * kscale,
        "cs_m": jnp.concatenate([cos_m, cos_m, sin_m, sin_m], axis=1),
        "dmat": dmat,
        "qdec": jnp.exp(log_g * (idx + 1.0)[None, :, None]),
        "kdec": jnp.exp(log_g * (SEQ_BLOCK - 1.0 - idx)[None, :, None]),
    }


def _with_rotated(w):
    half = MLA_ROPE // 2
    r = w[..., -MLA_ROPE:]
    return jnp.concatenate([w, -r[..., half:], r[..., :half]], axis=-1)


def kernel(x, meta, norm_mix_g, norm_ffn_g, ret_w_in, ret_gn_g, ret_w_o, mla_norm_kv_g, mla_w_kv_a, mla_kv_a_norm_g, mla_w_kv_b, mla_w_q_a, mla_q_a_norm_g, mla_w_q_b, mla_w_o, ffn_w1, ffn_w3, ffn_w2, final_g):
    B, S, D = x.shape
    L = PAD + N_META + S
    assert (L - HEAD_BLOCK) % SEQ_BLOCK == 0 and L % 4 == 0
    tl = L // 4
    n_ret = ret_w_in.shape[0]
    depth = norm_mix_g.shape[0]
    tabs = _tables(L)
    bf = lambda w: w.astype(BF16)
    row = lambda g: g.reshape(1, -1)

    h = jnp.concatenate([jnp.zeros((B, PAD, D), x.dtype),
                         jnp.broadcast_to(meta[None].astype(x.dtype), (B, N_META, D)), x], axis=1)
    kv = kr = None
    for layer in range(depth):
        if layer < n_ret:
            proj = _proj(h, row(norm_mix_g[layer]), bf(ret_w_in[layer]), tl)
            y = _retention(proj, tabs, row(ret_gn_g[layer]))
            wo = ret_w_o[layer]
        else:
            j = layer - n_ret
            if j == 0:
                kv, kr = _mla_kv(h, row(mla_norm_kv_g), bf(_with_rotated(mla_w_kv_a)),
                                 row(mla_kv_a_norm_g), bf(mla_w_kv_b), tabs["cs_m"], tl)
            wqb = mla_w_q_b[j].reshape(-1, MLA_HEADS, MLA_NOPE + MLA_ROPE)
            q = _mla_q(h, row(norm_mix_g[layer]), bf(mla_w_q_a[j]), row(mla_q_a_norm_g[j]),
                       bf(_with_rotated(wqb).reshape(wqb.shape[0], -1)), tabs["cs_m"], tl)
            y = _attention(q, kv, kr)
            wo = mla_w_o[j]
        h = _out_ffn(h, y, bf(wo), row(norm_ffn_g[layer]), bf(ffn_w1[layer]), bf(ffn_w3[layer]),
                     bf(ffn_w2[layer]), tl, final_g=row(final_g) if layer == depth - 1 else None)
    return h[:, PAD + N_META:]
```

```python
import functools

import jax
import jax.numpy as jnp
from jax import lax
from jax.experimental import pallas as pl
from jax.experimental.pallas import tpu as pltpu

F32 = jnp.float32
BF16 = jnp.bfloat16

CHUNK = 64
N_META = 16
Q_BLOCK = 128
PAD = Q_BLOCK - N_META
RMS_EPS = 1e-6
GN_EPS = 1e-5
ROPE_THETA = 10000.0
MASK_VALUE = -1e30

RET_HEADS = 4
RET_QK_DIM = 256
RET_V_DIM = 512

MLA_HEADS = 8
MLA_NOPE = 128
MLA_ROPE = 64
MLA_V = 128
MLA_KV_RANK = 512

FFN_CHUNK = 256
HEAD_BLOCK = 128
SEQ_BLOCK = 256
VMEM_LIMIT = 56 * 1024 * 1024


def _params(n_grid):
    return pltpu.CompilerParams(
        dimension_semantics=("arbitrary",) * n_grid, vmem_limit_bytes=VMEM_LIMIT)


def _resident(shape):
    return pl.BlockSpec(shape, lambda *_: (0,) * len(shape), pipeline_mode=pl.Buffered(1))


def _rms(x, g):
    return x * lax.rsqrt(jnp.mean(x * x, axis=-1, keepdims=True) + RMS_EPS) * g


def _dot(a, b):
    return jnp.dot(a, b, preferred_element_type=F32)


def _dot_nt(a, b):
    return lax.dot_general(a, b, (((1,), (1,)), ((), ())), preferred_element_type=F32)


def _dot_tn(a, b):
    return lax.dot_general(a, b, (((0,), (0,)), ((), ())), preferred_element_type=F32)


def _proj_kernel(h_ref, g_ref, w_ref, o_ref, *, tn):
    hn = _rms(h_ref[0], g_ref[...]).astype(BF16)
    for c in range(w_ref.shape[1] // tn):
        o_ref[0, :, c * tn:(c + 1) * tn] = _dot(hn, w_ref[:, c * tn:(c + 1) * tn]).astype(BF16)


def _proj(h, g, w, tl, tn=1024):
    B, L, D = h.shape
    N = w.shape[1]
    return pl.pallas_call(
        functools.partial(_proj_kernel, tn=tn),
        grid=(B, L // tl),
        in_specs=[pl.BlockSpec((1, tl, D), lambda b, i: (b, i, 0)),
                  _resident((1, D)), _resident((D, N))],
        out_specs=pl.BlockSpec((1, tl, N), lambda b, i: (b, i, 0)),
        out_shape=jax.ShapeDtypeStruct((B, L, N), BF16),
        compiler_params=_params(2), name="ret_proj",
    )(h, g, w)


def _rope_halves(x, c, s):
    half = x.shape[1] // 2
    x1, x2 = x[:, :half], x[:, half:]
    return jnp.concatenate([x1 * c - x2 * s, x1 * s + x2 * c], axis=1)


def _retention_kernel(q_ref, k_ref, v_ref, gate_ref, cq_ref, sq_ref, ck_ref, sk_ref,
                      dmat_ref, qdec_ref, kdec_ref, gn_ref, o_ref, s_ref):
    s_ref[...] = jnp.zeros_like(s_ref)

    def block(r0, n):
        rows = pl.ds(r0, n)
        off = SEQ_BLOCK - n
        q = _rope_halves(q_ref[0, rows, :].astype(F32), cq_ref[rows, :], sq_ref[rows, :])
        k = _rope_halves(k_ref[0, rows, :].astype(F32), ck_ref[rows, :], sk_ref[rows, :])
        v = v_ref[0, rows, :]
        sc = _dot_nt(q.astype(BF16), k.astype(BF16)) * dmat_ref[0, :n, :n]
        state = s_ref[...]
        qd = (q * qdec_ref[0, :n, :]).astype(BF16)
        o = _dot(sc.astype(BF16), v) + _dot(qd, state.astype(BF16))
        kd = (k * kdec_ref[0, off:off + n, :]).astype(BF16)
        s_ref[...] = state * qdec_ref[0, n - 1:n, :] + _dot_tn(kd, v)
        mu = jnp.mean(o, axis=-1, keepdims=True)
        d = o - mu
        var = jnp.mean(d * d, axis=-1, keepdims=True)
        on = d * lax.rsqrt(var + GN_EPS) * gn_ref[...]
        g = gate_ref[0, rows, :].astype(F32)
        o_ref[0, rows, :] = (g * jax.nn.sigmoid(g) * on).astype(BF16)

    block(0, HEAD_BLOCK)

    def body(i, carry):
        block(pl.multiple_of(HEAD_BLOCK + i * SEQ_BLOCK, HEAD_BLOCK), SEQ_BLOCK)
        return carry

    lax.fori_loop(0, (q_ref.shape[1] - HEAD_BLOCK) // SEQ_BLOCK, body, 0)


def _retention(proj, tabs, gn):
    B, L, _ = proj.shape
    H, DK, DV = RET_HEADS, RET_QK_DIM, RET_V_DIM
    nq = H * DK // DK
    nv = 2 * H * DK // DV
    ng = nv + H
    return pl.pallas_call(
        _retention_kernel,
        grid=(B, H),
        in_specs=[pl.BlockSpec((1, L, DK), lambda b, h: (b, 0, h)),
                  pl.BlockSpec((1, L, DK), lambda b, h: (b, 0, nq + h)),
                  pl.BlockSpec((1, L, DV), lambda b, h: (b, 0, nv + h)),
                  pl.BlockSpec((1, L, DV), lambda b, h: (b, 0, ng + h)),
                  _resident((L, DK // 2)), _resident((L, DK // 2)),
                  _resident((L, DK // 2)), _resident((L, DK // 2)),
                  pl.BlockSpec((1, SEQ_BLOCK, SEQ_BLOCK), lambda b, h: (h, 0, 0)),
                  pl.BlockSpec((1, SEQ_BLOCK, 1), lambda b, h: (h, 0, 0)),
                  pl.BlockSpec((1, SEQ_BLOCK, 1), lambda b, h: (h, 0, 0)),
                  pl.BlockSpec((1, DV), lambda b, h: (0, h))],
        out_specs=pl.BlockSpec((1, L, DV), lambda b, h: (b, 0, h)),
        out_shape=jax.ShapeDtypeStruct((B, L, H * DV), BF16),
        scratch_shapes=[pltpu.VMEM((DK, DV), F32)],
        compiler_params=_params(2), name="retention",
    )(proj, proj, proj, proj, tabs["cos_q"], tabs["sin_q"], tabs["cos_k"], tabs["sin_k"],
      tabs["dmat"], tabs["qdec"], tabs["kdec"], gn)


def _out_ffn_kernel(h_ref, y_ref, wo_ref, g_ref, w1_ref, w3_ref, w2_ref, *rest, final):
    if final:
        gf_ref, o_ref = rest
    else:
        (o_ref,) = rest
    h1 = h_ref[0] + _dot(y_ref[0], wo_ref[...])
    hn = _rms(h1, g_ref[...]).astype(BF16)
    acc = h1
    for c in range(w1_ref.shape[1] // FFN_CHUNK):
        cols = slice(c * FFN_CHUNK, (c + 1) * FFN_CHUNK)
        a = _dot(hn, w1_ref[:, cols])
        u = (a * jax.nn.sigmoid(a) * _dot(hn, w3_ref[:, cols])).astype(BF16)
        acc = acc + _dot(u, w2_ref[cols, :])
    if final:
        acc = _rms(acc, gf_ref[...])
    o_ref[0] = acc


def _out_ffn(h, y, wo, g, w1, w3, w2, tl, final_g=None):
    B, L, D = h.shape
    Y = y.shape[2]
    F = w1.shape[1]
    final = final_g is not None
    in_specs = [pl.BlockSpec((1, tl, D), lambda b, i: (b, i, 0)),
                pl.BlockSpec((1, tl, Y), lambda b, i: (b, i, 0)),
                _resident((Y, D)), _resident((1, D)),
                _resident((D, F)), _resident((D, F)), _resident((F, D))]
    args = [h, y, wo, g, w1, w3, w2]
    if final:
        in_specs.append(_resident((1, D)))
        args.append(final_g)
    return pl.pallas_call(
        functools.partial(_out_ffn_kernel, final=final),
        grid=(B, L // tl),
        in_specs=in_specs,
        out_specs=pl.BlockSpec((1, tl, D), lambda b, i: (b, i, 0)),
        out_shape=jax.ShapeDtypeStruct((B, L, D), F32),
        compiler_params=_params(2), name="out_ffn",
    )(*args)


def _rope_pair(t, cs):
    u = t * cs
    return u + pltpu.roll(u, MLA_ROPE, 1)


def _mla_kv_kernel(h_ref, g_ref, wa_ref, ga_ref, wb_ref, cs_ref, kv_ref, kr_ref):
    hn = _rms(h_ref[0], g_ref[...]).astype(BF16)
    kva = _dot(hn, wa_ref[...])
    ckv = _rms(kva[:, :MLA_KV_RANK], ga_ref[...]).astype(BF16)
    rp = _rope_pair(kva[:, MLA_KV_RANK:], cs_ref[...])
    lane = lax.broadcasted_iota(jnp.int32, rp.shape, 1)
    kr_ref[0] = jnp.where(lane < MLA_ROPE, rp, 0.0).astype(BF16)
    kv_ref[0] = _dot(ckv, wb_ref[...]).astype(BF16)


def _mla_kv(h, g, wa, ga, wb, cs, tl):
    B, L, D = h.shape
    NA, NB = wa.shape[1], wb.shape[1]
    return pl.pallas_call(
        _mla_kv_kernel,
        grid=(B, L // tl),
        in_specs=[pl.BlockSpec((1, tl, D), lambda b, i: (b, i, 0)),
                  _resident((1, D)), _resident((D, NA)), _resident((1, MLA_KV_RANK)),
                  _resident((MLA_KV_RANK, NB)),
                  pl.BlockSpec((tl, 2 * MLA_ROPE), lambda b, i: (i, 0))],
        out_specs=[pl.BlockSpec((1, tl, NB), lambda b, i: (b, i, 0)),
                   pl.BlockSpec((1, tl, 2 * MLA_ROPE), lambda b, i: (b, i, 0))],
        out_shape=[jax.ShapeDtypeStruct((B, L, NB), BF16),
                   jax.ShapeDtypeStruct((B, L, 2 * MLA_ROPE), BF16)],
        compiler_params=_params(2), name="mla_kv",
    )(h, g, wa, ga, wb, cs)


def _mla_q_kernel(h_ref, g_ref, wa_ref, ga_ref, wb_ref, cs_ref, q_ref):
    hn = _rms(h_ref[0], g_ref[...]).astype(BF16)
    cq = _rms(_dot(hn, wa_ref[...]), ga_ref[...]).astype(BF16)
    scale = (MLA_NOPE + MLA_ROPE) ** -0.5
    width = MLA_NOPE + 2 * MLA_ROPE
    for hd in range(MLA_HEADS):
        qh = _dot(cq, wb_ref[:, hd * width:(hd + 1) * width])
        q_ref[0, :, hd * width:hd * width + MLA_NOPE] = (qh[:, :MLA_NOPE] * scale).astype(BF16)
        q_ref[0, :, hd * width + MLA_NOPE:(hd + 1) * width] = (
            _rope_pair(qh[:, MLA_NOPE:], cs_ref[...]) * scale).astype(BF16)


def _mla_q(h, g, wa, ga, wb, cs, tl):
    B, L, D = h.shape
    R, N = wb.shape
    return pl.pallas_call(
        _mla_q_kernel,
        grid=(B, L // tl),
        in_specs=[pl.BlockSpec((1, tl, D), lambda b, i: (b, i, 0)),
                  _resident((1, D)), _resident((D, R)), _resident((1, R)), _resident((R, N)),
                  pl.BlockSpec((tl, 2 * MLA_ROPE), lambda b, i: (i, 0))],
        out_specs=pl.BlockSpec((1, tl, N), lambda b, i: (b, i, 0)),
        out_shape=jax.ShapeDtypeStruct((B, L, N), BF16),
        compiler_params=_params(2), name="mla_q",
    )(h, g, wa, ga, wb, cs)


def _attention_kernel(q_ref, kn_ref, kr_ref, v_ref, o_ref, kcat_ref):
    kcat_ref[:, :MLA_NOPE] = kn_ref[0]
    kcat_ref[:, MLA_NOPE:] = kr_ref[0]

    def chunk_mask(n):
        row = lax.broadcasted_iota(jnp.int32, (n, n), 0)
        col = lax.broadcasted_iota(jnp.int32, (n, n), 1)
        return (col // CHUNK) <= (row // CHUNK)

    def valid_mask(n):
        return lax.broadcasted_iota(jnp.int32, (n, HEAD_BLOCK), 1) >= PAD

    def qblock(r0, n, pieces):
        q = q_ref[0, r0:r0 + n, :]
        scores = []
        for k0, kl, mask in pieces:
            s = _dot_nt(q, kcat_ref[k0:k0 + kl, :])
            scores.append(s if mask is None else jnp.where(mask, s, MASK_VALUE))
        m = functools.reduce(jnp.maximum, [jnp.max(s, axis=1, keepdims=True) for s in scores])
        l = jnp.zeros((n, 1), F32)
        acc = jnp.zeros((n, MLA_V), F32)
        for s, (k0, kl, _) in zip(scores, pieces):
            p = jnp.exp(s - m)
            l = l + jnp.sum(p, axis=1, keepdims=True)
            acc = acc + _dot(p.astype(BF16), v_ref[0, k0:k0 + kl, :])
        o_ref[0, r0:r0 + n, :] = (acc / l).astype(BF16)

    qblock(0, HEAD_BLOCK, [(0, HEAD_BLOCK, chunk_mask(HEAD_BLOCK) & valid_mask(HEAD_BLOCK))])
    for i in range((q_ref.shape[1] - HEAD_BLOCK) // SEQ_BLOCK):
        r0 = HEAD_BLOCK + i * SEQ_BLOCK
        pieces = [(0, HEAD_BLOCK, valid_mask(SEQ_BLOCK))]
        if i > 0:
            pieces.append((HEAD_BLOCK, r0 - HEAD_BLOCK, None))
        pieces.append((r0, SEQ_BLOCK, chunk_mask(SEQ_BLOCK)))
        qblock(r0, SEQ_BLOCK, pieces)


def _attention(q, kv, kr):
    B, L, _ = q.shape
    H = MLA_HEADS
    W = MLA_NOPE + 2 * MLA_ROPE
    return pl.pallas_call(
        _attention_kernel,
        grid=(B, H),
        in_specs=[pl.BlockSpec((1, L, W), lambda b, h: (b, 0, h)),
                  pl.BlockSpec((1, L, MLA_NOPE), lambda b, h: (b, 0, 2 * h)),
                  pl.BlockSpec((1, L, 2 * MLA_ROPE), lambda b, h: (b, 0, 0)),
                  pl.BlockSpec((1, L, MLA_V), lambda b, h: (b, 0, 2 * h + 1))],
        out_specs=pl.BlockSpec((1, L, MLA_V), lambda b, h: (b, 0, h)),
        out_shape=jax.ShapeDtypeStruct((B, L, H * MLA_V), BF16),
        scratch_shapes=[pltpu.VMEM((L, W), BF16)],
        compiler_params=_params(2), name="attention",
    )(q, kv, kr, kv)


def _rope_tables(pos, dim):
    inv = 1.0 / (ROPE_THETA ** (jnp.arange(0, dim, 2, dtype=F32) / dim))
    ang = pos.astype(F32)[:, None] * inv[None, :]
    return jnp.cos(ang), jnp.sin(ang)


def _tables(L):
    slot = jnp.arange(L)
    pos = slot - PAD
    valid = (slot >= PAD).astype(F32)[:, None]
    cos_r, sin_r = _rope_tables(pos, RET_QK_DIM)
    kscale = valid * (RET_QK_DIM ** -0.5)
    cos_m, sin_m = _rope_tables(pos, MLA_ROPE)
    log_g = jnp.log1p(-jnp.exp2(-5.0 - jnp.arange(RET_HEADS, dtype=F32)))[:, None, None]
    idx = jnp.arange(SEQ_BLOCK, dtype=F32)
    ci = jnp.arange(SEQ_BLOCK) // CHUNK
    dist = jnp.abs(idx[:, None] - idx[None, :])
    dmat = jnp.where((ci[None, :] <= ci[:, None])[None], jnp.exp(log_g * dist[None]), 0.0)
    return {
        "cos_q": cos_r, "sin_q": sin_r, "cos_k": cos_r * kscale, "sin_k": sin_r * kscale,
        "cs_m": jnp.concatenate([cos_m, cos_m, sin_m, sin_m], axis=1),
        "dmat": dmat,
        "qdec": jnp.exp(log_g * (idx + 1.0)[None, :, None]),
        "kdec": jnp.exp(log_g * (SEQ_BLOCK - 1.0 - idx)[None, :, None]),
    }


def _with_rotated(w):
    half = MLA_ROPE // 2
    r = w[..., -MLA_ROPE:]
    return jnp.concatenate([w, -r[..., half:], r[..., :half]], axis=-1)


def kernel(x, meta, norm_mix_g, norm_ffn_g, ret_w_in, ret_gn_g, ret_w_o, mla_norm_kv_g, mla_w_kv_a, mla_kv_a_norm_g, mla_w_kv_b, mla_w_q_a, mla_q_a_norm_g, mla_w_q_b, mla_w_o, ffn_w1, ffn_w3, ffn_w2, final_g):
    B, S, D = x.shape
    L = PAD + N_META + S
    assert (L - HEAD_BLOCK) % SEQ_BLOCK == 0 and L % 4 == 0
    tl = L // 4
    n_ret = ret_w_in.shape[0]
    depth = norm_mix_g.shape[0]
    tabs = _tables(L)
    bf = lambda w: w.astype(BF16)
    row = lambda g: g.reshape(1, -1)

    h = jnp.concatenate([jnp.zeros((B, PAD, D), x.dtype),
                         jnp.broadcast_to(meta[None].astype(x.dtype), (B, N_META, D)), x], axis=1)
    kv = kr = None
    for layer in range(depth):
        if layer < n_ret:
            proj = _proj(h, row(norm_mix_g[layer]), bf(ret_w_in[layer]), tl)
            y = _retention(proj, tabs, row(ret_gn_g[layer]))
            wo = ret_w_o[layer]
        else:
            j = layer - n_ret
            if j == 0:
                kv, kr = _mla_kv(h, row(mla_norm_kv_g), bf(_with_rotated(mla_w_kv_a)),
                                 row(mla_kv_a_norm_g), bf(mla_w_kv_b), tabs["cs_m"], tl)
            wqb = mla_w_q_b[j].reshape(-1, MLA_HEADS, MLA_NOPE + MLA_ROPE)
            q = _mla_q(h, row(norm_mix_g[layer]), bf(mla_w_q_a[j]), row(mla_q_a_norm_g[j]),
                       bf(_with_rotated(wqb).reshape(wqb.shape[0], -1)), tabs["cs_m"], tl)
            y = _attention(q, kv, kr)
            wo = mla_w_o[j]
        h = _out_ffn(h, y, bf(wo), row(norm_ffn_g[layer]), bf(ffn_w1[layer]), bf(ffn_w3[layer]),
                     bf(ffn_w2[layer]), tl, final_g=row(final_g) if layer == depth - 1 else None)
    return h[:, PAD + N_META:]
```

```python
import functools

import jax
import jax.numpy as jnp
from jax import lax
from jax.experimental import pallas as pl
from jax.experimental.pallas import tpu as pltpu

F32 = jnp.float32
BF16 = jnp.bfloat16

CHUNK = 64
N_META = 16
HEAD_ROWS = 128
PAD = HEAD_ROWS - N_META
RMS_EPS = 1e-6
GN_EPS = 1e-5
ROPE_THETA = 10000.0
MASK_VALUE = -1e30

RET_HEADS = 4
RET_QK_DIM = 256
RET_V_DIM = 512

MLA_HEADS = 8
MLA_NOPE = 128
MLA_ROPE = 64
MLA_V = 128
MLA_KV_RANK = 512

FFN_CHUNK = 256
SEQ_BLOCK = 256
ROW_BLOCK = 512
VMEM_LIMIT = 56 * 1024 * 1024


def _params(n_grid):
    return pltpu.CompilerParams(
        dimension_semantics=("arbitrary",) * n_grid, vmem_limit_bytes=VMEM_LIMIT)


def _resident(shape):
    return pl.BlockSpec(shape, lambda *_: (0,) * len(shape), pipeline_mode=pl.Buffered(1))


def _rows(tl, width):
    return pl.BlockSpec((1, tl, width), lambda b, i: (b, i, 0))


def _rms(x, g):
    return x * lax.rsqrt(jnp.mean(x * x, axis=-1, keepdims=True) + RMS_EPS) * g


def _silu(x):
    return x * jax.nn.sigmoid(x)


def _dot(a, b):
    return jnp.dot(a, b, preferred_element_type=F32)


def _dot_nt(a, b):
    return lax.dot_general(a, b, (((1,), (1,)), ((), ())), preferred_element_type=F32)


def _dot_tn(a, b):
    return lax.dot_general(a, b, (((0,), (0,)), ((), ())), preferred_element_type=F32)


def _ret_proj_kernel(h_ref, g_ref, w_ref, cq_ref, sq_ref, ck_ref, sk_ref, o_ref):
    hn = _rms(h_ref[0], g_ref[...]).astype(BF16)
    dk, half = RET_QK_DIM, RET_QK_DIM // 2
    n_qk = RET_HEADS * dk
    n_v = RET_HEADS * RET_V_DIM
    for hd in range(2 * RET_HEADS):
        c0 = hd * dk
        c_ref, s_ref = (cq_ref, sq_ref) if hd < RET_HEADS else (ck_ref, sk_ref)
        x = _dot(hn, w_ref[:, c0:c0 + dk])
        x1, x2 = x[:, :half], x[:, half:]
        c, s = c_ref[...], s_ref[...]
        o_ref[0, :, c0:c0 + half] = (x1 * c - x2 * s).astype(BF16)
        o_ref[0, :, c0 + half:c0 + dk] = (x1 * s + x2 * c).astype(BF16)
    for c0 in range(2 * n_qk, 2 * n_qk + n_v, RET_V_DIM):
        o_ref[0, :, c0:c0 + RET_V_DIM] = _dot(hn, w_ref[:, c0:c0 + RET_V_DIM]).astype(BF16)
    for c0 in range(2 * n_qk + n_v, 2 * n_qk + 2 * n_v, RET_V_DIM):
        o_ref[0, :, c0:c0 + RET_V_DIM] = _silu(_dot(hn, w_ref[:, c0:c0 + RET_V_DIM])).astype(BF16)


def _ret_proj(h, g, w, tabs, tl):
    B, L, D = h.shape
    N = w.shape[1]
    tab = pl.BlockSpec((tl, RET_QK_DIM // 2), lambda b, i: (i, 0))
    return pl.pallas_call(
        _ret_proj_kernel,
        grid=(B, L // tl),
        in_specs=[_rows(tl, D), _resident((1, D)), _resident((D, N)), tab, tab, tab, tab],
        out_specs=_rows(tl, N),
        out_shape=jax.ShapeDtypeStruct((B, L, N), BF16),
        compiler_params=_params(2), name="ret_proj",
    )(h, g, w, tabs["cos_q"], tabs["sin_q"], tabs["cos_k"], tabs["sin_k"])


def _retention_kernel(q_ref, k_ref, v_ref, sg_ref, s0_ref, dmat_ref, qdec_ref, kdec_ref, gn_ref,
                      o_ref, *maybe_state_ref, n):
    n_blocks = q_ref.shape[1] // n
    state = s0_ref[0]
    for i in range(n_blocks):
        rows = slice(i * n, (i + 1) * n)
        q, k, v = q_ref[0, rows, :], k_ref[0, rows, :], v_ref[0, rows, :]
        sc = _dot_nt(q, k) * dmat_ref[0]
        o = _dot(sc.astype(BF16), v) + qdec_ref[0] * _dot(q, state.astype(BF16))
        if maybe_state_ref or i + 1 < n_blocks:
            kd = (k.astype(F32) * kdec_ref[0]).astype(BF16)
            state = state * qdec_ref[0, n - 1:n, :] + _dot_tn(kd, v)
        mu = jnp.mean(o, axis=-1, keepdims=True)
        d = o - mu
        var = jnp.mean(d * d, axis=-1, keepdims=True)
        on = d * lax.rsqrt(var + GN_EPS) * gn_ref[...]
        o_ref[0, rows, :] = (sg_ref[0, rows, :].astype(F32) * on).astype(BF16)
    if maybe_state_ref:
        maybe_state_ref[0][0, 0] = state


def _retention(proj, s0, dec, gn, n, emit_state):
    B, L, _ = proj.shape
    H, DK, DV = RET_HEADS, RET_QK_DIM, RET_V_DIM
    nk = H
    nv = 2 * H * DK // DV
    ng = nv + H
    per_head = lambda shape: pl.BlockSpec((1,) + shape, lambda b, h: (h, 0, 0))
    out_specs = [pl.BlockSpec((1, L, DV), lambda b, h: (b, 0, h))]
    out_shape = [jax.ShapeDtypeStruct((B, L, H * DV), BF16)]
    if emit_state:
        out_specs.append(pl.BlockSpec((1, 1, DK, DV), lambda b, h: (b, h, 0, 0)))
        out_shape.append(jax.ShapeDtypeStruct((B, H, DK, DV), F32))
    return pl.pallas_call(
        functools.partial(_retention_kernel, n=n),
        grid=(B, H),
        in_specs=[pl.BlockSpec((1, L, DK), lambda b, h: (b, 0, h)),
                  pl.BlockSpec((1, L, DK), lambda b, h: (b, 0, nk + h)),
                  pl.BlockSpec((1, L, DV), lambda b, h: (b, 0, nv + h)),
                  pl.BlockSpec((1, L, DV), lambda b, h: (b, 0, ng + h)),
                  per_head((DK, DV)), per_head((n, n)), per_head((n, 1)), per_head((n, 1)),
                  pl.BlockSpec((1, DV), lambda b, h: (0, h))],
        out_specs=out_specs, out_shape=out_shape,
        compiler_params=_params(2), name="retention",
    )(proj, proj, proj, proj, s0, dec["dmat"], dec["qdec"], dec["kdec"], gn)


def _out_ffn_kernel(h_ref, y_ref, wo_ref, g_ref, w1_ref, w3_ref, w2_ref, *rest, final):
    if final:
        gf_ref, o_ref = rest
    else:
        (o_ref,) = rest
    h1 = h_ref[0] + _dot(y_ref[0], wo_ref[...])
    hn = _rms(h1, g_ref[...]).astype(BF16)
    acc = h1
    for c in range(w1_ref.shape[1] // FFN_CHUNK):
        cols = slice(c * FFN_CHUNK, (c + 1) * FFN_CHUNK)
        u = (_silu(_dot(hn, w1_ref[:, cols])) * _dot(hn, w3_ref[:, cols])).astype(BF16)
        acc = acc + _dot(u, w2_ref[cols, :])
    if final:
        acc = _rms(acc, gf_ref[...])
    o_ref[0] = acc


def _out_ffn(h, y, wo, g, w1, w3, w2, tl, final_g=None):
    B, L, D = h.shape
    Y = y.shape[2]
    F = w1.shape[1]
    final = final_g is not None
    in_specs = [_rows(tl, D), _rows(tl, Y), _resident((Y, D)), _resident((1, D)),
                _resident((D, F)), _resident((D, F)), _resident((F, D))]
    args = [h, y, wo, g, w1, w3, w2]
    if final:
        in_specs.append(_resident((1, D)))
        args.append(final_g)
    return pl.pallas_call(
        functools.partial(_out_ffn_kernel, final=final),
        grid=(B, L // tl),
        in_specs=in_specs,
        out_specs=_rows(tl, D),
        out_shape=jax.ShapeDtypeStruct((B, L, D), F32),
        compiler_params=_params(2), name="out_ffn",
    )(*args)


def _rope_pair(t, cs):
    u = t * cs
    return u + pltpu.roll(u, MLA_ROPE, 1)


def _mla_kv_kernel(h_ref, g_ref, wa_ref, ga_ref, wb_ref, cs_ref, kv_ref, kr_ref):
    hn = _rms(h_ref[0], g_ref[...]).astype(BF16)
    kva = _dot(hn, wa_ref[...])
    ckv = _rms(kva[:, :MLA_KV_RANK], ga_ref[...]).astype(BF16)
    rp = _rope_pair(kva[:, MLA_KV_RANK:], cs_ref[...])
    lane = lax.broadcasted_iota(jnp.int32, rp.shape, 1)
    kr_ref[0] = jnp.where(lane < MLA_ROPE, rp, 0.0).astype(BF16)
    kv_ref[0] = _dot(ckv, wb_ref[...]).astype(BF16)


def _mla_kv(h, g, wa, ga, wb, cs, tl):
    B, L, D = h.shape
    NA, NB = wa.shape[1], wb.shape[1]
    return pl.pallas_call(
        _mla_kv_kernel,
        grid=(B, L // tl),
        in_specs=[_rows(tl, D), _resident((1, D)), _resident((D, NA)),
                  _resident((1, MLA_KV_RANK)), _resident((MLA_KV_RANK, NB)),
                  pl.BlockSpec((tl, 2 * MLA_ROPE), lambda b, i: (i, 0))],
        out_specs=[_rows(tl, NB), _rows(tl, 2 * MLA_ROPE)],
        out_shape=[jax.ShapeDtypeStruct((B, L, NB), BF16),
                   jax.ShapeDtypeStruct((B, L, 2 * MLA_ROPE), BF16)],
        compiler_params=_params(2), name="mla_kv",
    )(h, g, wa, ga, wb, cs)


def _mla_q_kernel(h_ref, g_ref, wa_ref, ga_ref, wb_ref, cs_ref, q_ref):
    hn = _rms(h_ref[0], g_ref[...]).astype(BF16)
    cq = _rms(_dot(hn, wa_ref[...]), ga_ref[...]).astype(BF16)
    scale = (MLA_NOPE + MLA_ROPE) ** -0.5
    width = MLA_NOPE + 2 * MLA_ROPE
    for hd in range(MLA_HEADS):
        qh = _dot(cq, wb_ref[:, hd * width:(hd + 1) * width])
        q_ref[0, :, hd * width:hd * width + MLA_NOPE] = (qh[:, :MLA_NOPE] * scale).astype(BF16)
        q_ref[0, :, hd * width + MLA_NOPE:(hd + 1) * width] = (
            _rope_pair(qh[:, MLA_NOPE:], cs_ref[...]) * scale).astype(BF16)


def _mla_q(h, g, wa, ga, wb, cs, tl):
    B, L, D = h.shape
    R, N = wb.shape
    return pl.pallas_call(
        _mla_q_kernel,
        grid=(B, L // tl),
        in_specs=[_rows(tl, D), _resident((1, D)), _resident((D, R)), _resident((1, R)),
                  _resident((R, N)), pl.BlockSpec((tl, 2 * MLA_ROPE), lambda b, i: (i, 0))],
        out_specs=_rows(tl, N),
        out_shape=jax.ShapeDtypeStruct((B, L, N), BF16),
        compiler_params=_params(2), name="mla_q",
    )(h, g, wa, ga, wb, cs)


def _attention_kernel(q_ref, kn_ref, kr_ref, v_ref, knh_ref, krh_ref, vh_ref, o_ref, kcat_ref):
    kcat_ref[:HEAD_ROWS, :MLA_NOPE] = knh_ref[0]
    kcat_ref[:HEAD_ROWS, MLA_NOPE:] = krh_ref[0]
    kcat_ref[HEAD_ROWS:, :MLA_NOPE] = kn_ref[0]
    kcat_ref[HEAD_ROWS:, MLA_NOPE:] = kr_ref[0]
    n = SEQ_BLOCK
    row = lax.broadcasted_iota(jnp.int32, (n, n), 0)
    col = lax.broadcasted_iota(jnp.int32, (n, n), 1)
    chunk_mask = (col // CHUNK) <= (row // CHUNK)
    head_mask = lax.broadcasted_iota(jnp.int32, (n, HEAD_ROWS), 1) >= PAD

    for i in range(q_ref.shape[1] // n):
        r0 = i * n
        q = q_ref[0, r0:r0 + n, :]
        pieces = [(jnp.where(head_mask, _dot_nt(q, kcat_ref[:HEAD_ROWS, :]), MASK_VALUE), vh_ref[0])]
        if i > 0:
            pieces.append((_dot_nt(q, kcat_ref[HEAD_ROWS:HEAD_ROWS + r0, :]), v_ref[0, :r0, :]))
        pieces.append((jnp.where(chunk_mask, _dot_nt(q, kcat_ref[HEAD_ROWS + r0:HEAD_ROWS + r0 + n, :]),
                                 MASK_VALUE), v_ref[0, r0:r0 + n, :]))
        m = functools.reduce(jnp.maximum, [jnp.max(s, axis=1, keepdims=True) for s, _ in pieces])
        l = jnp.zeros((n, 1), F32)
        acc = jnp.zeros((n, MLA_V), F32)
        for s, v in pieces:
            p = jnp.exp(s - m)
            l = l + jnp.sum(p, axis=1, keepdims=True)
            acc = acc + _dot(p.astype(BF16), v)
        o_ref[0, r0:r0 + n, :] = (acc / l).astype(BF16)


def _attention(q, kv, kr, kv_head, kr_head):
    B, L, _ = q.shape
    H = MLA_HEADS
    W = MLA_NOPE + 2 * MLA_ROPE
    return pl.pallas_call(
        _attention_kernel,
        grid=(B, H),
        in_specs=[pl.BlockSpec((1, L, W), lambda b, h: (b, 0, h)),
                  pl.BlockSpec((1, L, MLA_NOPE), lambda b, h: (b, 0, 2 * h)),
                  pl.BlockSpec((1, L, 2 * MLA_ROPE), lambda b, h: (b, 0, 0)),
                  pl.BlockSpec((1, L, MLA_V), lambda b, h: (b, 0, 2 * h + 1)),
                  pl.BlockSpec((1, HEAD_ROWS, MLA_NOPE), lambda b, h: (0, 0, 2 * h)),
                  pl.BlockSpec((1, HEAD_ROWS, 2 * MLA_ROPE), lambda b, h: (0, 0, 0)),
                  pl.BlockSpec((1, HEAD_ROWS, MLA_V), lambda b, h: (0, 0, 2 * h + 1))],
        out_specs=pl.BlockSpec((1, L, MLA_V), lambda b, h: (b, 0, h)),
        out_shape=jax.ShapeDtypeStruct((B, L, H * MLA_V), BF16),
        scratch_shapes=[pltpu.VMEM((HEAD_ROWS + L, W), BF16)],
        compiler_params=_params(2), name="attention",
    )(q, kv, kr, kv, kv_head, kr_head, kv_head)


def _rope_tables(pos, dim):
    inv = 1.0 / (ROPE_THETA ** (jnp.arange(0, dim, 2, dtype=F32) / dim))
    ang = pos.astype(F32)[:, None] * inv[None, :]
    return jnp.cos(ang), jnp.sin(ang)


def _stream_tables(slot):
    pos = slot - PAD
    kscale = (slot >= PAD).astype(F32)[:, None] * (RET_QK_DIM ** -0.5)
    cos_r, sin_r = _rope_tables(pos, RET_QK_DIM)
    cos_m, sin_m = _rope_tables(pos, MLA_ROPE)
    return {"cos_q": cos_r, "sin_q": sin_r, "cos_k": cos_r * kscale, "sin_k": sin_r * kscale,
            "cs_m": jnp.concatenate([cos_m, cos_m, sin_m, sin_m], axis=1)}


def _decay_tables(n):
    log_g = jnp.log1p(-jnp.exp2(-5.0 - jnp.arange(RET_HEADS, dtype=F32)))[:, None, None]
    idx = jnp.arange(n, dtype=F32)
    ci = jnp.arange(n) // CHUNK
    dist = jnp.abs(idx[:, None] - idx[None, :])
    return {"dmat": jnp.where((ci[None, :] <= ci[:, None])[None], jnp.exp(log_g * dist[None]), 0.0),
            "qdec": jnp.exp(log_g * (idx + 1.0)[None, :, None]),
            "kdec": jnp.exp(log_g * (n - 1.0 - idx)[None, :, None])}


def _with_rotated(w):
    half = MLA_ROPE // 2
    r = w[..., -MLA_ROPE:]
    return jnp.concatenate([w, -r[..., half:], r[..., :half]], axis=-1)


def kernel(x, meta, norm_mix_g, norm_ffn_g, ret_w_in, ret_gn_g, ret_w_o, mla_norm_kv_g, mla_w_kv_a, mla_kv_a_norm_g, mla_w_kv_b, mla_w_q_a, mla_q_a_norm_g, mla_w_q_b, mla_w_o, ffn_w1, ffn_w3, ffn_w2, final_g):
    B, S, D = x.shape
    assert S % ROW_BLOCK == 0 and S % SEQ_BLOCK == 0
    n_ret = ret_w_in.shape[0]
    depth = norm_mix_g.shape[0]
    bf = lambda w: w.astype(BF16)
    row = lambda g: g.reshape(1, -1)

    streams = [
        [jnp.concatenate([jnp.zeros((1, PAD, D), x.dtype), meta[None].astype(x.dtype)], axis=1),
         _stream_tables(jnp.arange(HEAD_ROWS)), _decay_tables(HEAD_ROWS), HEAD_ROWS, HEAD_ROWS],
        [x, _stream_tables(HEAD_ROWS + jnp.arange(S)), _decay_tables(SEQ_BLOCK), ROW_BLOCK, SEQ_BLOCK],
    ]
    kv_head = kr_head = None
    h = None
    for layer in range(depth):
        ffn = (row(norm_ffn_g[layer]), bf(ffn_w1[layer]), bf(ffn_w3[layer]), bf(ffn_w2[layer]))
        if layer < n_ret:
            w_in, w_o, gn = bf(ret_w_in[layer]), bf(ret_w_o[layer]), row(ret_gn_g[layer])
            state = jnp.zeros((RET_HEADS, RET_QK_DIM, RET_V_DIM), F32)
            for is_main, st in enumerate(streams):
                hs, tabs, dec, tl, n = st
                proj = _ret_proj(hs, row(norm_mix_g[layer]), w_in, tabs, tl)
                if is_main:
                    (y,) = _retention(proj, state, dec, gn, n, emit_state=False)
                else:
                    y, state = _retention(proj, state, dec, gn, n, emit_state=True)
                    state = state[0]
                st[0] = _out_ffn(hs, y, w_o, *ffn, tl)
            continue
        j = layer - n_ret
        if j == 0:
            kv_w = (row(mla_norm_kv_g), bf(_with_rotated(mla_w_kv_a)), row(mla_kv_a_norm_g),
                    bf(mla_w_kv_b))
            hs, tabs, _, tl, _ = streams[0]
            kv_head, kr_head = _mla_kv(hs, *kv_w, tabs["cs_m"], tl)
            h, tabs, _, tl, _ = streams[1]
            kv, kr = _mla_kv(h, *kv_w, tabs["cs_m"], tl)
        wqb = mla_w_q_b[j].reshape(-1, MLA_HEADS, MLA_NOPE + MLA_ROPE)
        q = _mla_q(h, row(norm_mix_g[layer]), bf(mla_w_q_a[j]), row(mla_q_a_norm_g[j]),
                   bf(_with_rotated(wqb).reshape(wqb.shape[0], -1)), tabs["cs_m"], tl)
        y = _attention(q, kv, kr, kv_head, kr_head)
        h = _out_ffn(h, y, bf(mla_w_o[j]), *ffn, tl,
                     final_g=row(final_g) if layer == depth - 1 else None)
    return h
```

```python
import functools

import jax
import jax.numpy as jnp
from jax import lax
from jax.experimental import pallas as pl
from jax.experimental.pallas import tpu as pltpu

F32 = jnp.float32
BF16 = jnp.bfloat16

CHUNK = 64
N_META = 16
HEAD_ROWS = 128
PAD = HEAD_ROWS - N_META
RMS_EPS = 1e-6
GN_EPS = 1e-5
ROPE_THETA = 10000.0
MASK_VALUE = -1e30

RET_HEADS = 4
RET_QK_DIM = 256
RET_V_DIM = 512

MLA_HEADS = 8
MLA_NOPE = 128
MLA_ROPE = 64
MLA_V = 128
MLA_KV_RANK = 512

FFN_CHUNK = 256
SEQ_BLOCK = 256
ATT_BLOCK = 512
SUB_ROWS = 256
PROJ_SUB_ROWS = 256
ROW_BLOCK = 512
VMEM_LIMIT = 56 * 1024 * 1024


def _params(n_grid):
    return pltpu.CompilerParams(
        dimension_semantics=("arbitrary",) * n_grid, vmem_limit_bytes=VMEM_LIMIT)


def _resident(shape):
    return pl.BlockSpec(shape, lambda *_: (0,) * len(shape), pipeline_mode=pl.Buffered(1))


def _rows(tl, width):
    return pl.BlockSpec((1, tl, width), lambda b, i: (b, i, 0))


def _rms(x, g):
    return x * lax.rsqrt(jnp.mean(x * x, axis=-1, keepdims=True) + RMS_EPS) * g


def _silu(x):
    return x * jax.nn.sigmoid(x)


def _dot(a, b):
    return jnp.dot(a, b, preferred_element_type=F32)


def _dot_nt(a, b):
    return lax.dot_general(a, b, (((1,), (1,)), ((), ())), preferred_element_type=F32)


def _dot_tn(a, b):
    return lax.dot_general(a, b, (((0,), (0,)), ((), ())), preferred_element_type=F32)


def _ret_proj_kernel(h_ref, g_ref, w_ref, cq_ref, sq_ref, ck_ref, sk_ref, o_ref):
    dk, half = RET_QK_DIM, RET_QK_DIM // 2
    n_qk = RET_HEADS * dk
    n_v = RET_HEADS * RET_V_DIM
    tl = h_ref.shape[1]
    sub = min(tl, PROJ_SUB_ROWS)
    for r0 in range(0, tl, sub):
        rows = slice(r0, r0 + sub)
        hn = _rms(h_ref[0, rows, :], g_ref[...]).astype(BF16)
        for hd in range(2 * RET_HEADS):
            c0 = hd * dk
            c_ref, s_ref = (cq_ref, sq_ref) if hd < RET_HEADS else (ck_ref, sk_ref)
            x = _dot(hn, w_ref[:, c0:c0 + dk])
            x1, x2 = x[:, :half], x[:, half:]
            c, s = c_ref[rows, :], s_ref[rows, :]
            o_ref[0, rows, c0:c0 + half] = (x1 * c - x2 * s).astype(BF16)
            o_ref[0, rows, c0 + half:c0 + dk] = (x1 * s + x2 * c).astype(BF16)
        for c0 in range(2 * n_qk, 2 * n_qk + n_v, RET_V_DIM):
            o_ref[0, rows, c0:c0 + RET_V_DIM] = _dot(
                hn, w_ref[:, c0:c0 + RET_V_DIM]).astype(BF16)
        for c0 in range(2 * n_qk + n_v, 2 * n_qk + 2 * n_v, RET_V_DIM):
            o_ref[0, rows, c0:c0 + RET_V_DIM] = _silu(
                _dot(hn, w_ref[:, c0:c0 + RET_V_DIM])).astype(BF16)


def _ret_proj(h, g, w, tabs, tl):
    B, L, D = h.shape
    N = w.shape[1]
    tab = pl.BlockSpec((tl, RET_QK_DIM // 2), lambda b, i: (i, 0))
    return pl.pallas_call(
        _ret_proj_kernel,
        grid=(B, L // tl),
        in_specs=[_rows(tl, D), _resident((1, D)), _resident((D, N)), tab, tab, tab, tab],
        out_specs=_rows(tl, N),
        out_shape=jax.ShapeDtypeStruct((B, L, N), BF16),
        compiler_params=_params(2), name="ret_proj",
    )(h, g, w, tabs["cos_q"], tabs["sin_q"], tabs["cos_k"], tabs["sin_k"])


def _retention_kernel(q_ref, k_ref, v_ref, sg_ref, s0_ref, dmat_ref, qdec_ref, kdec_ref, gn_ref,
                      o_ref, *maybe_state_ref, n):
    n_blocks = q_ref.shape[1] // n
    state = s0_ref[0]
    for i in range(n_blocks):
        rows = slice(i * n, (i + 1) * n)
        q, k, v = q_ref[0, rows, :], k_ref[0, rows, :], v_ref[0, rows, :]
        sc = _dot_nt(q, k) * dmat_ref[0]
        o = _dot(sc.astype(BF16), v) + qdec_ref[0] * _dot(q, state.astype(BF16))
        if maybe_state_ref or i + 1 < n_blocks:
            kd = (k.astype(F32) * kdec_ref[0]).astype(BF16)
            state = state * qdec_ref[0, n - 1:n, :] + _dot_tn(kd, v)
        mu = jnp.mean(o, axis=-1, keepdims=True)
        d = o - mu
        var = jnp.mean(d * d, axis=-1, keepdims=True)
        on = d * lax.rsqrt(var + GN_EPS) * gn_ref[...]
        o_ref[0, rows, :] = (sg_ref[0, rows, :].astype(F32) * on).astype(BF16)
    if maybe_state_ref:
        maybe_state_ref[0][0, 0] = state


def _retention(proj, s0, dec, gn, n, emit_state):
    B, L, _ = proj.shape
    H, DK, DV = RET_HEADS, RET_QK_DIM, RET_V_DIM
    nk = H
    nv = 2 * H * DK // DV
    ng = nv + H
    per_head = lambda shape: pl.BlockSpec((1,) + shape, lambda b, h: (h, 0, 0))
    out_specs = [pl.BlockSpec((1, L, DV), lambda b, h: (b, 0, h))]
    out_shape = [jax.ShapeDtypeStruct((B, L, H * DV), BF16)]
    if emit_state:
        out_specs.append(pl.BlockSpec((1, 1, DK, DV), lambda b, h: (b, h, 0, 0)))
        out_shape.append(jax.ShapeDtypeStruct((B, H, DK, DV), F32))
    return pl.pallas_call(
        functools.partial(_retention_kernel, n=n),
        grid=(B, H),
        in_specs=[pl.BlockSpec((1, L, DK), lambda b, h: (b, 0, h)),
                  pl.BlockSpec((1, L, DK), lambda b, h: (b, 0, nk + h)),
                  pl.BlockSpec((1, L, DV), lambda b, h: (b, 0, nv + h)),
                  pl.BlockSpec((1, L, DV), lambda b, h: (b, 0, ng + h)),
                  per_head((DK, DV)), per_head((n, n)), per_head((n, 1)), per_head((n, 1)),
                  pl.BlockSpec((1, DV), lambda b, h: (0, h))],
        out_specs=out_specs, out_shape=out_shape,
        compiler_params=_params(2), name="retention",
    )(proj, proj, proj, proj, s0, dec["dmat"], dec["qdec"], dec["kdec"], gn)


def _out_ffn_kernel(h_ref, y_ref, wo_ref, g_ref, w1_ref, w3_ref, w2_ref, *rest, final):
    if final:
        gf_ref, o_ref = rest
    else:
        (o_ref,) = rest
    h1 = h_ref[0] + _dot(y_ref[0], wo_ref[...])
    hn = _rms(h1, g_ref[...]).astype(BF16)
    acc = h1
    for c in range(w1_ref.shape[1] // FFN_CHUNK):
        cols = slice(c * FFN_CHUNK, (c + 1) * FFN_CHUNK)
        u = (_silu(_dot(hn, w1_ref[:, cols])) * _dot(hn, w3_ref[:, cols])).astype(BF16)
        acc = acc + _dot(u, w2_ref[cols, :])
    if final:
        acc = _rms(acc, gf_ref[...])
    o_ref[0] = acc


def _out_ffn(h, y, wo, g, w1, w3, w2, tl, final_g=None):
    B, L, D = h.shape
    Y = y.shape[2]
    F = w1.shape[1]
    final = final_g is not None
    in_specs = [_rows(tl, D), _rows(tl, Y), _resident((Y, D)), _resident((1, D)),
                _resident((D, F)), _resident((D, F)), _resident((F, D))]
    args = [h, y, wo, g, w1, w3, w2]
    if final:
        in_specs.append(_resident((1, D)))
        args.append(final_g)
    return pl.pallas_call(
        functools.partial(_out_ffn_kernel, final=final),
        grid=(B, L // tl),
        in_specs=in_specs,
        out_specs=_rows(tl, D),
        out_shape=jax.ShapeDtypeStruct((B, L, D), F32),
        compiler_params=_params(2), name="out_ffn",
    )(*args)


def _rope_pair(t, cs):
    u = t * cs
    return u + pltpu.roll(u, MLA_ROPE, 1)


def _mla_kv_kernel(h_ref, g_ref, wa_ref, ga_ref, wb_ref, cs_ref, kv_ref, kr_ref):
    tl = h_ref.shape[1]
    sub = min(tl, SUB_ROWS)
    for r0 in range(0, tl, sub):
        rows = slice(r0, r0 + sub)
        hn = _rms(h_ref[0, rows, :], g_ref[...]).astype(BF16)
        kva = _dot(hn, wa_ref[...])
        ckv = _rms(kva[:, :MLA_KV_RANK], ga_ref[...]).astype(BF16)
        rp = _rope_pair(kva[:, MLA_KV_RANK:], cs_ref[rows, :])
        lane = lax.broadcasted_iota(jnp.int32, rp.shape, 1)
        kr_ref[0, rows, :] = jnp.where(lane < MLA_ROPE, rp, 0.0).astype(BF16)
        kv_ref[0, rows, :] = _dot(ckv, wb_ref[...]).astype(BF16)


def _mla_kv(h, g, wa, ga, wb, cs, tl):
    B, L, D = h.shape
    NA, NB = wa.shape[1], wb.shape[1]
    return pl.pallas_call(
        _mla_kv_kernel,
        grid=(B, L // tl),
        in_specs=[_rows(tl, D), _resident((1, D)), _resident((D, NA)),
                  _resident((1, MLA_KV_RANK)), _resident((MLA_KV_RANK, NB)),
                  pl.BlockSpec((tl, 2 * MLA_ROPE), lambda b, i: (i, 0))],
        out_specs=[_rows(tl, NB), _rows(tl, 2 * MLA_ROPE)],
        out_shape=[jax.ShapeDtypeStruct((B, L, NB), BF16),
                   jax.ShapeDtypeStruct((B, L, 2 * MLA_ROPE), BF16)],
        compiler_params=_params(2), name="mla_kv",
    )(h, g, wa, ga, wb, cs)


def _mla_q_kernel(h_ref, g_ref, wa_ref, ga_ref, wb_ref, cs_ref, q_ref):
    scale = (MLA_NOPE + MLA_ROPE) ** -0.5
    width = MLA_NOPE + 2 * MLA_ROPE
    tl = h_ref.shape[1]
    sub = min(tl, SUB_ROWS)
    for r0 in range(0, tl, sub):
        rows = slice(r0, r0 + sub)
        hn = _rms(h_ref[0, rows, :], g_ref[...]).astype(BF16)
        cq = _rms(_dot(hn, wa_ref[...]), ga_ref[...]).astype(BF16)
        cs = cs_ref[rows, :]
        for hd in range(MLA_HEADS):
            qh = _dot(cq, wb_ref[:, hd * width:(hd + 1) * width])
            q_ref[0, rows, hd * width:hd * width + MLA_NOPE] = (
                qh[:, :MLA_NOPE] * scale).astype(BF16)
            q_ref[0, rows, hd * width + MLA_NOPE:(hd + 1) * width] = (
                _rope_pair(qh[:, MLA_NOPE:], cs) * scale).astype(BF16)


def _mla_q(h, g, wa, ga, wb, cs, tl):
    B, L, D = h.shape
    R, N = wb.shape
    return pl.pallas_call(
        _mla_q_kernel,
        grid=(B, L // tl),
        in_specs=[_rows(tl, D), _resident((1, D)), _resident((D, R)), _resident((1, R)),
                  _resident((R, N)), pl.BlockSpec((tl, 2 * MLA_ROPE), lambda b, i: (i, 0))],
        out_specs=_rows(tl, N),
        out_shape=jax.ShapeDtypeStruct((B, L, N), BF16),
        compiler_params=_params(2), name="mla_q",
    )(h, g, wa, ga, wb, cs)


def _attention_kernel(q_ref, kn_ref, kr_ref, v_ref, knh_ref, krh_ref, vh_ref, o_ref, kcat_ref):
    kcat_ref[:HEAD_ROWS, :MLA_NOPE] = knh_ref[0]
    kcat_ref[:HEAD_ROWS, MLA_NOPE:] = krh_ref[0]
    kcat_ref[HEAD_ROWS:, :MLA_NOPE] = kn_ref[0]
    kcat_ref[HEAD_ROWS:, MLA_NOPE:] = kr_ref[0]
    n = ATT_BLOCK
    row = lax.broadcasted_iota(jnp.int32, (n, n), 0)
    col = lax.broadcasted_iota(jnp.int32, (n, n), 1)
    chunk_mask = (col // CHUNK) <= (row // CHUNK)
    head_mask = lax.broadcasted_iota(jnp.int32, (n, HEAD_ROWS), 1) >= PAD

    for i in range(q_ref.shape[1] // n):
        r0 = i * n
        q = q_ref[0, r0:r0 + n, :]
        pieces = [(jnp.where(head_mask, _dot_nt(q, kcat_ref[:HEAD_ROWS, :]), MASK_VALUE), vh_ref[0])]
        if i > 0:
            pieces.append((_dot_nt(q, kcat_ref[HEAD_ROWS:HEAD_ROWS + r0, :]), v_ref[0, :r0, :]))
        pieces.append((jnp.where(chunk_mask, _dot_nt(q, kcat_ref[HEAD_ROWS + r0:HEAD_ROWS + r0 + n, :]),
                                 MASK_VALUE), v_ref[0, r0:r0 + n, :]))
        m = functools.reduce(jnp.maximum, [jnp.max(s, axis=1, keepdims=True) for s, _ in pieces])
        l = jnp.zeros((n, 1), F32)
        acc = jnp.zeros((n, MLA_V), F32)
        for s, v in pieces:
            p = jnp.exp(s - m)
            l = l + jnp.sum(p, axis=1, keepdims=True)
            acc = acc + _dot(p.astype(BF16), v)
        o_ref[0, r0:r0 + n, :] = (acc / l).astype(BF16)


def _attention(q, kv, kr, kv_head, kr_head):
    B, L, _ = q.shape
    H = MLA_HEADS
    W = MLA_NOPE + 2 * MLA_ROPE
    return pl.pallas_call(
        _attention_kernel,
        grid=(B, H),
        in_specs=[pl.BlockSpec((1, L, W), lambda b, h: (b, 0, h)),
                  pl.BlockSpec((1, L, MLA_NOPE), lambda b, h: (b, 0, 2 * h)),
                  pl.BlockSpec((1, L, 2 * MLA_ROPE), lambda b, h: (b, 0, 0)),
                  pl.BlockSpec((1, L, MLA_V), lambda b, h: (b, 0, 2 * h + 1)),
                  pl.BlockSpec((1, HEAD_ROWS, MLA_NOPE), lambda b, h: (0, 0, 2 * h)),
                  pl.BlockSpec((1, HEAD_ROWS, 2 * MLA_ROPE), lambda b, h: (0, 0, 0)),
                  pl.BlockSpec((1, HEAD_ROWS, MLA_V), lambda b, h: (0, 0, 2 * h + 1))],
        out_specs=pl.BlockSpec((1, L, MLA_V), lambda b, h: (b, 0, h)),
        out_shape=jax.ShapeDtypeStruct((B, L, H * MLA_V), BF16),
        scratch_shapes=[pltpu.VMEM((HEAD_ROWS + L, W), BF16)],
        compiler_params=_params(2), name="attention",
    )(q, kv, kr, kv, kv_head, kr_head, kv_head)


def _rope_tables(pos, dim):
    inv = 1.0 / (ROPE_THETA ** (jnp.arange(0, dim, 2, dtype=F32) / dim))
    ang = pos.astype(F32)[:, None] * inv[None, :]
    return jnp.cos(ang), jnp.sin(ang)


def _stream_tables(slot):
    pos = slot - PAD
    kscale = (slot >= PAD).astype(F32)[:, None] * (RET_QK_DIM ** -0.5)
    cos_r, sin_r = _rope_tables(pos, RET_QK_DIM)
    cos_m, sin_m = _rope_tables(pos, MLA_ROPE)
    return {"cos_q": cos_r, "sin_q": sin_r, "cos_k": cos_r * kscale, "sin_k": sin_r * kscale,
            "cs_m": jnp.concatenate([cos_m, cos_m, sin_m, sin_m], axis=1)}


def _decay_tables(n):
    log_g = jnp.log1p(-jnp.exp2(-5.0 - jnp.arange(RET_HEADS, dtype=F32)))[:, None, None]
    idx = jnp.arange(n, dtype=F32)
    ci = jnp.arange(n) // CHUNK
    dist = jnp.abs(idx[:, None] - idx[None, :])
    return {"dmat": jnp.where((ci[None, :] <= ci[:, None])[None], jnp.exp(log_g * dist[None]), 0.0),
            "qdec": jnp.exp(log_g * (idx + 1.0)[None, :, None]),
            "kdec": jnp.exp(log_g * (n - 1.0 - idx)[None, :, None])}


def _with_rotated(w):
    half = MLA_ROPE // 2
    r = w[..., -MLA_ROPE:]
    return jnp.concatenate([w, -r[..., half:], r[..., :half]], axis=-1)


def kernel(x, meta, norm_mix_g, norm_ffn_g, ret_w_in, ret_gn_g, ret_w_o, mla_norm_kv_g, mla_w_kv_a, mla_kv_a_norm_g, mla_w_kv_b, mla_w_q_a, mla_q_a_norm_g, mla_w_q_b, mla_w_o, ffn_w1, ffn_w3, ffn_w2, final_g):
    B, S, D = x.shape
    assert S % ROW_BLOCK == 0 and S % SEQ_BLOCK == 0
    n_ret = ret_w_in.shape[0]
    depth = norm_mix_g.shape[0]
    bf = lambda w: w.astype(BF16)
    row = lambda g: g.reshape(1, -1)

    streams = [
        [jnp.concatenate([jnp.zeros((1, PAD, D), x.dtype), meta[None].astype(x.dtype)], axis=1),
         _stream_tables(jnp.arange(HEAD_ROWS)), _decay_tables(HEAD_ROWS), HEAD_ROWS, HEAD_ROWS],
        [x, _stream_tables(HEAD_ROWS + jnp.arange(S)), _decay_tables(SEQ_BLOCK), ROW_BLOCK, SEQ_BLOCK],
    ]
    kv_head = kr_head = None
    h = None
    for layer in range(depth):
        ffn = (row(norm_ffn_g[layer]), bf(ffn_w1[layer]), bf(ffn_w3[layer]), bf(ffn_w2[layer]))
        if layer < n_ret:
            w_in, w_o, gn = bf(ret_w_in[layer]), bf(ret_w_o[layer]), row(ret_gn_g[layer])
            state = jnp.zeros((RET_HEADS, RET_QK_DIM, RET_V_DIM), F32)
            for is_main, st in enumerate(streams):
                hs, tabs, dec, tl, n = st
                proj = _ret_proj(hs, row(norm_mix_g[layer]), w_in, tabs, tl)
                if is_main:
                    (y,) = _retention(proj, state, dec, gn, n, emit_state=False)
                else:
                    y, state = _retention(proj, state, dec, gn, n, emit_state=True)
                    state = state[0]
                st[0] = _out_ffn(hs, y, w_o, *ffn, tl)
            continue
        j = layer - n_ret
        if j == 0:
            kv_w = (row(mla_norm_kv_g), bf(_with_rotated(mla_w_kv_a)), row(mla_kv_a_norm_g),
                    bf(mla_w_kv_b))
            hs, tabs, _, tl, _ = streams[0]
            kv_head, kr_head = _mla_kv(hs, *kv_w, tabs["cs_m"], tl)
            h, tabs, _, tl, _ = streams[1]
            kv, kr = _mla_kv(h, *kv_w, tabs["cs_m"], tl)
        wqb = mla_w_q_b[j].reshape(-1, MLA_HEADS, MLA_NOPE + MLA_ROPE)
        q = _mla_q(h, row(norm_mix_g[layer]), bf(mla_w_q_a[j]), row(mla_q_a_norm_g[j]),
                   bf(_with_rotated(wqb).reshape(wqb.shape[0], -1)), tabs["cs_m"], tl)
        y = _attention(q, kv, kr, kv_head, kr_head)
        h = _out_ffn(h, y, bf(mla_w_o[j]), *ffn, tl,
                     final_g=row(final_g) if layer == depth - 1 else None)
    return h
```

```python
import functools

import jax
import jax.numpy as jnp
from jax import lax
from jax.experimental import pallas as pl
from jax.experimental.pallas import tpu as pltpu

F32 = jnp.float32
BF16 = jnp.bfloat16

CHUNK = 64
N_META = 16
HEAD_ROWS = 128
PAD = HEAD_ROWS - N_META
RMS_EPS = 1e-6
GN_EPS = 1e-5
ROPE_THETA = 10000.0
MASK_VALUE = -1e30
LOG2_E = 1.4426950408889634

RET_HEADS = 4
RET_QK_DIM = 256
RET_V_DIM = 512

MLA_HEADS = 8
MLA_NOPE = 128
MLA_ROPE = 64
MLA_V = 128
MLA_KV_RANK = 512

FFN_CHUNK = 256
SEQ_BLOCK = 256
ATT_BLOCK = 512
SUB_ROWS = 256
PROJ_SUB_ROWS = 256
ROW_BLOCK = 512
VMEM_LIMIT = 56 * 1024 * 1024


def _params(n_grid):
    return pltpu.CompilerParams(
        dimension_semantics=("arbitrary",) * n_grid, vmem_limit_bytes=VMEM_LIMIT)


def _resident(shape):
    return pl.BlockSpec(shape, lambda *_: (0,) * len(shape), pipeline_mode=pl.Buffered(1))


def _rows(tl, width):
    return pl.BlockSpec((1, tl, width), lambda b, i: (b, i, 0))


def _rms(x, g):
    return x * lax.rsqrt(jnp.mean(x * x, axis=-1, keepdims=True) + RMS_EPS) * g


def _silu(x):
    return x * jax.nn.sigmoid(x)


def _dot(a, b):
    return jnp.dot(a, b, preferred_element_type=F32)


def _dot_nt(a, b):
    return lax.dot_general(a, b, (((1,), (1,)), ((), ())), preferred_element_type=F32)


def _dot_tn(a, b):
    return lax.dot_general(a, b, (((0,), (0,)), ((), ())), preferred_element_type=F32)


def _ret_proj_kernel(h_ref, g_ref, w_ref, cq_ref, sq_ref, ck_ref, sk_ref, o_ref):
    dk, half = RET_QK_DIM, RET_QK_DIM // 2
    n_qk = RET_HEADS * dk
    n_v = RET_HEADS * RET_V_DIM
    tl = h_ref.shape[1]
    sub = min(tl, PROJ_SUB_ROWS)
    for r0 in range(0, tl, sub):
        rows = slice(r0, r0 + sub)
        hn = _rms(h_ref[0, rows, :], g_ref[...]).astype(BF16)
        for hd in range(2 * RET_HEADS):
            c0 = hd * dk
            c_ref, s_ref = (cq_ref, sq_ref) if hd < RET_HEADS else (ck_ref, sk_ref)
            x = _dot(hn, w_ref[:, c0:c0 + dk])
            x1, x2 = x[:, :half], x[:, half:]
            c, s = c_ref[rows, :], s_ref[rows, :]
            o_ref[0, rows, c0:c0 + half] = (x1 * c - x2 * s).astype(BF16)
            o_ref[0, rows, c0 + half:c0 + dk] = (x1 * s + x2 * c).astype(BF16)
        for c0 in range(2 * n_qk, 2 * n_qk + n_v, RET_V_DIM):
            o_ref[0, rows, c0:c0 + RET_V_DIM] = _dot(
                hn, w_ref[:, c0:c0 + RET_V_DIM]).astype(BF16)
        for c0 in range(2 * n_qk + n_v, 2 * n_qk + 2 * n_v, RET_V_DIM):
            o_ref[0, rows, c0:c0 + RET_V_DIM] = _silu(
                _dot(hn, w_ref[:, c0:c0 + RET_V_DIM])).astype(BF16)


def _ret_proj(h, g, w, tabs, tl):
    B, L, D = h.shape
    N = w.shape[1]
    tab = pl.BlockSpec((tl, RET_QK_DIM // 2), lambda b, i: (i, 0))
    return pl.pallas_call(
        _ret_proj_kernel,
        grid=(B, L // tl),
        in_specs=[_rows(tl, D), _resident((1, D)), _resident((D, N)), tab, tab, tab, tab],
        out_specs=_rows(tl, N),
        out_shape=jax.ShapeDtypeStruct((B, L, N), BF16),
        compiler_params=_params(2), name="ret_proj",
    )(h, g, w, tabs["cos_q"], tabs["sin_q"], tabs["cos_k"], tabs["sin_k"])


def _retention_kernel(q_ref, k_ref, v_ref, sg_ref, s0_ref, dmat_ref, qdec_ref, kdec_ref, gn_ref,
                      o_ref, *maybe_state_ref, n):
    n_blocks = q_ref.shape[1] // n
    state = s0_ref[0]
    for i in range(n_blocks):
        rows = slice(i * n, (i + 1) * n)
        q, k, v = q_ref[0, rows, :], k_ref[0, rows, :], v_ref[0, rows, :]
        sc = _dot_nt(q, k) * dmat_ref[0]
        o = _dot(sc.astype(BF16), v) + qdec_ref[0] * _dot(q, state.astype(BF16))
        if maybe_state_ref or i + 1 < n_blocks:
            kd = (k.astype(F32) * kdec_ref[0]).astype(BF16)
            state = state * qdec_ref[0, n - 1:n, :] + _dot_tn(kd, v)
        mu = jnp.mean(o, axis=-1, keepdims=True)
        d = o - mu
        var = jnp.mean(d * d, axis=-1, keepdims=True)
        on = d * lax.rsqrt(var + GN_EPS) * gn_ref[...]
        o_ref[0, rows, :] = (sg_ref[0, rows, :].astype(F32) * on).astype(BF16)
    if maybe_state_ref:
        maybe_state_ref[0][0, 0] = state


def _retention(proj, s0, dec, gn, n, emit_state):
    B, L, _ = proj.shape
    H, DK, DV = RET_HEADS, RET_QK_DIM, RET_V_DIM
    nk = H
    nv = 2 * H * DK // DV
    ng = nv + H
    per_head = lambda shape: pl.BlockSpec((1,) + shape, lambda b, h: (h, 0, 0))
    out_specs = [pl.BlockSpec((1, L, DV), lambda b, h: (b, 0, h))]
    out_shape = [jax.ShapeDtypeStruct((B, L, H * DV), BF16)]
    if emit_state:
        out_specs.append(pl.BlockSpec((1, 1, DK, DV), lambda b, h: (b, h, 0, 0)))
        out_shape.append(jax.ShapeDtypeStruct((B, H, DK, DV), F32))
    return pl.pallas_call(
        functools.partial(_retention_kernel, n=n),
        grid=(B, H),
        in_specs=[pl.BlockSpec((1, L, DK), lambda b, h: (b, 0, h)),
                  pl.BlockSpec((1, L, DK), lambda b, h: (b, 0, nk + h)),
                  pl.BlockSpec((1, L, DV), lambda b, h: (b, 0, nv + h)),
                  pl.BlockSpec((1, L, DV), lambda b, h: (b, 0, ng + h)),
                  per_head((DK, DV)), per_head((n, n)), per_head((n, 1)), per_head((n, 1)),
                  pl.BlockSpec((1, DV), lambda b, h: (0, h))],
        out_specs=out_specs, out_shape=out_shape,
        compiler_params=_params(2), name="retention",
    )(proj, proj, proj, proj, s0, dec["dmat"], dec["qdec"], dec["kdec"], gn)


def _out_ffn_kernel(h_ref, y_ref, wo_ref, g_ref, w1_ref, w3_ref, w2_ref, *rest, final):
    if final:
        gf_ref, o_ref = rest
    else:
        (o_ref,) = rest
    h1 = h_ref[0] + _dot(y_ref[0], wo_ref[...])
    hn = _rms(h1, g_ref[...]).astype(BF16)
    acc = h1
    for c in range(w1_ref.shape[1] // FFN_CHUNK):
        cols = slice(c * FFN_CHUNK, (c + 1) * FFN_CHUNK)
        u = (_silu(_dot(hn, w1_ref[:, cols])) * _dot(hn, w3_ref[:, cols])).astype(BF16)
        acc = acc + _dot(u, w2_ref[cols, :])
    if final:
        acc = _rms(acc, gf_ref[...])
    o_ref[0] = acc


def _out_ffn(h, y, wo, g, w1, w3, w2, tl, final_g=None):
    B, L, D = h.shape
    Y = y.shape[2]
    F = w1.shape[1]
    final = final_g is not None
    in_specs = [_rows(tl, D), _rows(tl, Y), _resident((Y, D)), _resident((1, D)),
                _resident((D, F)), _resident((D, F)), _resident((F, D))]
    args = [h, y, wo, g, w1, w3, w2]
    if final:
        in_specs.append(_resident((1, D)))
        args.append(final_g)
    return pl.pallas_call(
        functools.partial(_out_ffn_kernel, final=final),
        grid=(B, L // tl),
        in_specs=in_specs,
        out_specs=_rows(tl, D),
        out_shape=jax.ShapeDtypeStruct((B, L, D), F32),
        compiler_params=_params(2), name="out_ffn",
    )(*args)


def _rope_pair(t, cs):
    u = t * cs
    return u + pltpu.roll(u, MLA_ROPE, 1)


def _mla_kv_kernel(h_ref, g_ref, wa_ref, ga_ref, wb_ref, cs_ref, kv_ref, kr_ref):
    tl = h_ref.shape[1]
    sub = min(tl, SUB_ROWS)
    for r0 in range(0, tl, sub):
        rows = slice(r0, r0 + sub)
        hn = _rms(h_ref[0, rows, :], g_ref[...]).astype(BF16)
        kva = _dot(hn, wa_ref[...])
        ckv = _rms(kva[:, :MLA_KV_RANK], ga_ref[...]).astype(BF16)
        rp = _rope_pair(kva[:, MLA_KV_RANK:], cs_ref[rows, :])
        lane = lax.broadcasted_iota(jnp.int32, rp.shape, 1)
        kr_ref[0, rows, :] = jnp.where(lane < MLA_ROPE, rp, 0.0).astype(BF16)
        kv_ref[0, rows, :] = _dot(ckv, wb_ref[...]).astype(BF16)


def _mla_kv(h, g, wa, ga, wb, cs, tl):
    B, L, D = h.shape
    NA, NB = wa.shape[1], wb.shape[1]
    return pl.pallas_call(
        _mla_kv_kernel,
        grid=(B, L // tl),
        in_specs=[_rows(tl, D), _resident((1, D)), _resident((D, NA)),
                  _resident((1, MLA_KV_RANK)), _resident((MLA_KV_RANK, NB)),
                  pl.BlockSpec((tl, 2 * MLA_ROPE), lambda b, i: (i, 0))],
        out_specs=[_rows(tl, NB), _rows(tl, 2 * MLA_ROPE)],
        out_shape=[jax.ShapeDtypeStruct((B, L, NB), BF16),
                   jax.ShapeDtypeStruct((B, L, 2 * MLA_ROPE), BF16)],
        compiler_params=_params(2), name="mla_kv",
    )(h, g, wa, ga, wb, cs)


def _mla_q_kernel(h_ref, g_ref, wa_ref, ga_ref, wb_ref, cs_ref, q_ref):
    scale = (MLA_NOPE + MLA_ROPE) ** -0.5 * LOG2_E
    width = MLA_NOPE + 2 * MLA_ROPE
    tl = h_ref.shape[1]
    sub = min(tl, SUB_ROWS)
    for r0 in range(0, tl, sub):
        rows = slice(r0, r0 + sub)
        hn = _rms(h_ref[0, rows, :], g_ref[...]).astype(BF16)
        cq = _rms(_dot(hn, wa_ref[...]), ga_ref[...]).astype(BF16)
        cs = cs_ref[rows, :]
        for hd in range(MLA_HEADS):
            qh = _dot(cq, wb_ref[:, hd * width:(hd + 1) * width])
            q_ref[0, rows, hd * width:hd * width + MLA_NOPE] = (
                qh[:, :MLA_NOPE] * scale).astype(BF16)
            q_ref[0, rows, hd * width + MLA_NOPE:(hd + 1) * width] = (
                _rope_pair(qh[:, MLA_NOPE:], cs) * scale).astype(BF16)


def _mla_q(h, g, wa, ga, wb, cs, tl):
    B, L, D = h.shape
    R, N = wb.shape
    return pl.pallas_call(
        _mla_q_kernel,
        grid=(B, L // tl),
        in_specs=[_rows(tl, D), _resident((1, D)), _resident((D, R)), _resident((1, R)),
                  _resident((R, N)), pl.BlockSpec((tl, 2 * MLA_ROPE), lambda b, i: (i, 0))],
        out_specs=_rows(tl, N),
        out_shape=jax.ShapeDtypeStruct((B, L, N), BF16),
        compiler_params=_params(2), name="mla_q",
    )(h, g, wa, ga, wb, cs)


def _attention_kernel(q_ref, kn_ref, kr_ref, v_ref, knh_ref, krh_ref, vh_ref, o_ref, kcat_ref):
    kcat_ref[:HEAD_ROWS, :MLA_NOPE] = knh_ref[0]
    kcat_ref[:HEAD_ROWS, MLA_NOPE:] = krh_ref[0]
    kcat_ref[HEAD_ROWS:, :MLA_NOPE] = kn_ref[0]
    kcat_ref[HEAD_ROWS:, MLA_NOPE:] = kr_ref[0]
    n = ATT_BLOCK
    row = lax.broadcasted_iota(jnp.int32, (n, n), 0)
    col = lax.broadcasted_iota(jnp.int32, (n, n), 1)
    chunk_mask = (col // CHUNK) <= (row // CHUNK)
    head_mask = lax.broadcasted_iota(jnp.int32, (n, HEAD_ROWS), 1) >= PAD

    for i in range(q_ref.shape[1] // n):
        r0 = i * n
        q = q_ref[0, r0:r0 + n, :]
        pieces = [(jnp.where(head_mask, _dot_nt(q, kcat_ref[:HEAD_ROWS, :]), MASK_VALUE), vh_ref[0])]
        if i > 0:
            pieces.append((_dot_nt(q, kcat_ref[HEAD_ROWS:HEAD_ROWS + r0, :]), v_ref[0, :r0, :]))
        pieces.append((jnp.where(chunk_mask, _dot_nt(q, kcat_ref[HEAD_ROWS + r0:HEAD_ROWS + r0 + n, :]),
                                 MASK_VALUE), v_ref[0, r0:r0 + n, :]))
        m = functools.reduce(jnp.maximum, [jnp.max(s, axis=1, keepdims=True) for s, _ in pieces])
        l = jnp.zeros((n, 1), F32)
        acc = jnp.zeros((n, MLA_V), F32)
        for s, v in pieces:
            p = jnp.exp2(s - m)
            l = l + jnp.sum(p, axis=1, keepdims=True)
            acc = acc + _dot(p.astype(BF16), v)
        o_ref[0, r0:r0 + n, :] = (acc / l).astype(BF16)


def _attention(q, kv, kr, kv_head, kr_head):
    B, L, _ = q.shape
    H = MLA_HEADS
    W = MLA_NOPE + 2 * MLA_ROPE
    return pl.pallas_call(
        _attention_kernel,
        grid=(B, H),
        in_specs=[pl.BlockSpec((1, L, W), lambda b, h: (b, 0, h)),
                  pl.BlockSpec((1, L, MLA_NOPE), lambda b, h: (b, 0, 2 * h)),
                  pl.BlockSpec((1, L, 2 * MLA_ROPE), lambda b, h: (b, 0, 0)),
                  pl.BlockSpec((1, L, MLA_V), lambda b, h: (b, 0, 2 * h + 1)),
                  pl.BlockSpec((1, HEAD_ROWS, MLA_NOPE), lambda b, h: (0, 0, 2 * h)),
                  pl.BlockSpec((1, HEAD_ROWS, 2 * MLA_ROPE), lambda b, h: (0, 0, 0)),
                  pl.BlockSpec((1, HEAD_ROWS, MLA_V), lambda b, h: (0, 0, 2 * h + 1))],
        out_specs=pl.BlockSpec((1, L, MLA_V), lambda b, h: (b, 0, h)),
        out_shape=jax.ShapeDtypeStruct((B, L, H * MLA_V), BF16),
        scratch_shapes=[pltpu.VMEM((HEAD_ROWS + L, W), BF16)],
        compiler_params=_params(2), name="attention",
    )(q, kv, kr, kv, kv_head, kr_head, kv_head)


def _rope_tables(pos, dim):
    inv = 1.0 / (ROPE_THETA ** (jnp.arange(0, dim, 2, dtype=F32) / dim))
    ang = pos.astype(F32)[:, None] * inv[None, :]
    return jnp.cos(ang), jnp.sin(ang)


def _stream_tables(slot):
    pos = slot - PAD
    kscale = (slot >= PAD).astype(F32)[:, None] * (RET_QK_DIM ** -0.5)
    cos_r, sin_r = _rope_tables(pos, RET_QK_DIM)
    cos_m, sin_m = _rope_tables(pos, MLA_ROPE)
    return {"cos_q": cos_r, "sin_q": sin_r, "cos_k": cos_r * kscale, "sin_k": sin_r * kscale,
            "cs_m": jnp.concatenate([cos_m, cos_m, sin_m, sin_m], axis=1)}


def _decay_tables(n):
    log_g = jnp.log1p(-jnp.exp2(-5.0 - jnp.arange(RET_HEADS, dtype=F32)))[:, None, None]
    idx = jnp.arange(n, dtype=F32)
    ci = jnp.arange(n) // CHUNK
    dist = jnp.abs(idx[:, None] - idx[None, :])
    return {"dmat": jnp.where((ci[None, :] <= ci[:, None])[None], jnp.exp(log_g * dist[None]), 0.0),
            "qdec": jnp.exp(log_g * (idx + 1.0)[None, :, None]),
            "kdec": jnp.exp(log_g * (n - 1.0 - idx)[None, :, None])}


def _with_rotated(w):
    half = MLA_ROPE // 2
    r = w[..., -MLA_ROPE:]
    return jnp.concatenate([w, -r[..., half:], r[..., :half]], axis=-1)


def kernel(x, meta, norm_mix_g, norm_ffn_g, ret_w_in, ret_gn_g, ret_w_o, mla_norm_kv_g, mla_w_kv_a, mla_kv_a_norm_g, mla_w_kv_b, mla_w_q_a, mla_q_a_norm_g, mla_w_q_b, mla_w_o, ffn_w1, ffn_w3, ffn_w2, final_g):
    B, S, D = x.shape
    assert S % ROW_BLOCK == 0 and S % SEQ_BLOCK == 0
    n_ret = ret_w_in.shape[0]
    depth = norm_mix_g.shape[0]
    bf = lambda w: w.astype(BF16)
    row = lambda g: g.reshape(1, -1)

    streams = [
        [jnp.concatenate([jnp.zeros((1, PAD, D), x.dtype), meta[None].astype(x.dtype)], axis=1),
         _stream_tables(jnp.arange(HEAD_ROWS)), _decay_tables(HEAD_ROWS), HEAD_ROWS, HEAD_ROWS],
        [x, _stream_tables(HEAD_ROWS + jnp.arange(S)), _decay_tables(SEQ_BLOCK), ROW_BLOCK, SEQ_BLOCK],
    ]
    kv_head = kr_head = None
    h = None
    for layer in range(depth):
        ffn = (row(norm_ffn_g[layer]), bf(ffn_w1[layer]), bf(ffn_w3[layer]), bf(ffn_w2[layer]))
        if layer < n_ret:
            w_in, w_o, gn = bf(ret_w_in[layer]), bf(ret_w_o[layer]), row(ret_gn_g[layer])
            state = jnp.zeros((RET_HEADS, RET_QK_DIM, RET_V_DIM), F32)
            for is_main, st in enumerate(streams):
                hs, tabs, dec, tl, n = st
                proj = _ret_proj(hs, row(norm_mix_g[layer]), w_in, tabs, tl)
                if is_main:
                    (y,) = _retention(proj, state, dec, gn, n, emit_state=False)
                else:
                    y, state = _retention(proj, state, dec, gn, n, emit_state=True)
                    state = state[0]
                st[0] = _out_ffn(hs, y, w_o, *ffn, tl)
            continue
        j = layer - n_ret
        if j == 0:
            kv_w = (row(mla_norm_kv_g), bf(_with_rotated(mla_w_kv_a)), row(mla_kv_a_norm_g),
                    bf(mla_w_kv_b))
            hs, tabs, _, tl, _ = streams[0]
            kv_head, kr_head = _mla_kv(hs, *kv_w, tabs["cs_m"], tl)
            h, tabs, _, tl, _ = streams[1]
            kv, kr = _mla_kv(h, *kv_w, tabs["cs_m"], tl)
        wqb = mla_w_q_b[j].reshape(-1, MLA_HEADS, MLA_NOPE + MLA_ROPE)
        q = _mla_q(h, row(norm_mix_g[layer]), bf(mla_w_q_a[j]), row(mla_q_a_norm_g[j]),
                   bf(_with_rotated(wqb).reshape(wqb.shape[0], -1)), tabs["cs_m"], tl)
        y = _attention(q, kv, kr, kv_head, kr_head)
        h = _out_ffn(h, y, bf(mla_w_o[j]), *ffn, tl,
                     final_g=row(final_g) if layer == depth - 1 else None)
    return h
```
